```python
import jax
import jax.numpy as jnp
from jax import lax
import numpy as np

D_MODEL = 1024
BATCH = 8
SEQ = 2048
DEPTH = 4
DEC_BATCH = 128
DEC_SEQ = 1
PAST_LEN = 2048
PAGE_SIZE = 128

N_MIXERS = 3
N_A = (DEPTH + 2) // 3
N_B = (DEPTH + 1) // 3
N_C = DEPTH // 3

NSA_HEADS = 16
NSA_KV_HEADS = 4
NSA_GROUP = NSA_HEADS // NSA_KV_HEADS
HEAD_DIM = D_MODEL // NSA_HEADS
Q_WIDTH = NSA_HEADS * HEAD_DIM
KV_WIDTH = NSA_KV_HEADS * HEAD_DIM
NSA_IN = Q_WIDTH + 6 * KV_WIDTH + 3 * NSA_HEADS
CMP_LEN = 32
CMP_STRIDE = 16
SLC_BLOCK = 64
TOP_N = 16
WINDOW = 512
Q_BLOCK = 128
FORCE_BONUS = 1e3
ATTN_SCALE = HEAD_DIM ** -0.5
NEG_INF = -1e30

GMLP_WIDTH = D_MODEL
GMLP_GROUPS = 8
GMLP_GROUP_DIM = GMLP_WIDTH // GMLP_GROUPS
GMLP_CHUNK = 128

HGRN_EXPAND = 128
HGRN_HEADS = D_MODEL // HGRN_EXPAND
HGRN_DK = HGRN_EXPAND
HGRN_DV = D_MODEL // HGRN_HEADS
HGRN_WIDTH = HGRN_HEADS * HGRN_DK
HGRN_CHUNK = 64

FFN_HIDDEN = ((8 * D_MODEL + 3 * 256 - 1) // (3 * 256)) * 256
DEEPNORM_ALPHA = (2 * DEPTH) ** 0.25
DEEPNORM_BETA = (8 * DEPTH) ** -0.25
LN_EPS = 1e-5

kernel_name = 'nsa_gmlp_hgrn2_hybrid_step'


def layer_norm(x, g, b):
    xf = x.astype(jnp.float32)
    mu = jnp.mean(xf, axis=-1, keepdims=True)
    var = jnp.mean(jnp.square(xf - mu), axis=-1, keepdims=True)
    return ((xf - mu) * lax.rsqrt(var + LN_EPS) * g.astype(jnp.float32) + b.astype(jnp.float32)).astype(x.dtype)


def rms_norm(x, g):
    xf = x.astype(jnp.float32)
    return xf * lax.rsqrt(jnp.mean(xf * xf, axis=-1, keepdims=True) + LN_EPS) * g.astype(jnp.float32)


def deepnorm(x, h, g, b):
    return layer_norm(DEEPNORM_ALPHA * x + h, g, b)


def swiglu_ffn(x, w_in, w_out):
    gate, up = jnp.split(x @ w_in, 2, axis=-1)
    return (jax.nn.silu(gate) * up) @ w_out


def masked_attend(q, k, v, mask):
    s = jnp.einsum('bqgrd,bkgd->bqgrk', q, k).astype(jnp.float32) * ATTN_SCALE
    p = jnp.where(mask, jax.nn.softmax(jnp.where(mask, s, NEG_INF), axis=-1), 0.0)
    return jnp.einsum('bqgrk,bkgd->bqgrd', p.astype(v.dtype), v), p


def compress(rows, w, pe):
    B, L, G, Dh = rows.shape
    lhs = rows.transpose(0, 2, 1, 3).reshape(B * G, L, Dh)
    out = lax.conv_general_dilated(lhs, w, (CMP_STRIDE,), 'VALID', dimension_numbers=('NWC', 'WIO', 'NWC'))
    out = out + jnp.einsum('pd,pde->e', pe, w)
    return out.reshape(B, G, out.shape[1], Dh).transpose(0, 2, 1, 3)


def to_blocks(rows):
    B, L, G, Dh = rows.shape
    ns = -(-L // SLC_BLOCK)
    rows = jnp.pad(rows, ((0, 0), (0, ns * SLC_BLOCK - L), (0, 0), (0, 0)))
    return rows.reshape(B, ns, SLC_BLOCK, G, Dh).transpose(0, 3, 1, 2, 4)


def cmp_to_slc(n_cmp, n_slc):
    c0 = jnp.arange(n_cmp)[:, None] * CMP_STRIDE
    s0 = jnp.arange(n_slc)[None, :] * SLC_BLOCK
    ov = jnp.minimum(c0 + CMP_LEN, s0 + SLC_BLOCK) - jnp.maximum(c0, s0)
    return jnp.clip(ov, 0, None).astype(jnp.float32) / CMP_STRIDE


def nsa_project(x, w_in, b_gate):
    B, T, _ = x.shape
    q, kv, g = jnp.split(x @ w_in, [Q_WIDTH, Q_WIDTH + 6 * KV_WIDTH], axis=-1)
    q = q.reshape(B, T, NSA_KV_HEADS, NSA_GROUP, HEAD_DIM)
    kv = kv.reshape(B, T, 3, 2, NSA_KV_HEADS, HEAD_DIM)
    gates = jax.nn.sigmoid(g + b_gate).reshape(B, T, 3, NSA_KV_HEADS, NSA_GROUP)
    return q, kv, gates


def nsa_summaries(rows_cmp, rows_slc, w_cmp, pe_cmp):
    kc = compress(rows_cmp[:, :, 0], w_cmp[0], pe_cmp[0])
    vc = compress(rows_cmp[:, :, 1], w_cmp[1], pe_cmp[1])
    n_cmp = kc.shape[1]
    c_end = jnp.arange(n_cmp) * CMP_STRIDE + CMP_LEN - 1
    ks_blk = to_blocks(rows_slc[:, :, 0])
    vs_blk = to_blocks(rows_slc[:, :, 1])
    return kc, vc, c_end, ks_blk, vs_blk, cmp_to_slc(n_cmp, ks_blk.shape[2])


def nsa_branches(q, gates, q_pos, kc, vc, c_end, ks_blk, vs_blk, kw, vw, w_pos, cmap):
    B, Tq, G, R, Dh = q.shape
    qp = q_pos[:, None]
    o_cmp, p_cmp = masked_attend(q, kc, vc, (c_end[None, :] <= qp)[None, :, None, None, :])
    n_slc = ks_blk.shape[2]
    blk = jnp.arange(n_slc)
    cur = (q_pos // SLC_BLOCK)[None, :, None, None]
    forced = (blk == 0) | (blk == cur) | (blk == cur - 1)
    importance = jnp.einsum('bqgrc,cs->bqgs', p_cmp, cmap)
    score = jnp.where(blk <= cur, importance + FORCE_BONUS * forced, -jnp.inf)
    n_top = min(TOP_N, n_slc)
    top_score, top_idx = lax.top_k(score, n_top)
    b_ix = jnp.arange(B)[:, None, None, None]
    g_ix = jnp.arange(G)[None, None, :, None]
    k_sel = ks_blk[b_ix, g_ix, top_idx]
    v_sel = vs_blk[b_ix, g_ix, top_idx]
    tok_pos = top_idx[..., None] * SLC_BLOCK + jnp.arange(SLC_BLOCK)
    sel_mask = jnp.isfinite(top_score)[..., None] & (tok_pos <= q_pos[None, :, None, None, None])
    sel_mask = sel_mask.reshape(B, Tq, G, 1, n_top * SLC_BLOCK)
    s = jnp.einsum('bqgrd,bqgnjd->bqgrnj', q, k_sel).astype(jnp.float32) * ATTN_SCALE
    p = jax.nn.softmax(jnp.where(sel_mask, s.reshape(B, Tq, G, R, -1), NEG_INF), axis=-1)
    o_slc = jnp.einsum('bqgrnj,bqgnjd->bqgrd', p.reshape(s.shape).astype(v_sel.dtype), v_sel)
    wp = w_pos[None, :]
    win_mask = (wp <= qp) & (wp > qp - WINDOW) & (wp >= 0)
    o_win, _ = masked_attend(q, kw, vw, win_mask[None, :, None, None, :])
    return (gates[:, :, 0, ..., None] * o_cmp + gates[:, :, 1, ..., None] * o_slc
            + gates[:, :, 2, ..., None] * o_win)


def nsa_prompt(x, w_in, b_gate, w_cmp, pe_cmp, w_out):
    B, T, _ = x.shape
    q, kv, gates = nsa_project(x, w_in, b_gate)
    kc, vc, c_end, ks_blk, vs_blk, cmap = nsa_summaries(kv[:, :, 0], kv[:, :, 1], w_cmp, pe_cmp)
    win_pad = jnp.pad(kv[:, :, 2], ((0, 0), (WINDOW, 0), (0, 0), (0, 0), (0, 0)))
    qb = Q_BLOCK if T % Q_BLOCK == 0 else T
    nb = T // qb

    def block(n):
        b, s0 = n // nb, (n % nb) * qb
        pick = lambda a: lax.dynamic_index_in_dim(a, b, 0, keepdims=True)
        rows = lambda a, size: lax.dynamic_slice_in_dim(pick(a), s0, size, axis=1)
        w_rows = rows(win_pad, qb + WINDOW)
        o = nsa_branches(rows(q, qb), rows(gates, qb), s0 + jnp.arange(qb), pick(kc), pick(vc), c_end,
                         pick(ks_blk), pick(vs_blk), w_rows[:, :, 0], w_rows[:, :, 1],
                         s0 - WINDOW + jnp.arange(qb + WINDOW), cmap)
        return o[0]

    o = lax.map(block, jnp.arange(B * nb))
    return o.reshape(B, T, Q_WIDTH) @ w_out, kv


def nsa_sample(x, cache_cmp, cache_slc, cache_win, page_table, w_in, b_gate, w_cmp, pe_cmp, w_out):
    B, T, _ = x.shape
    past = page_table.shape[1] * PAGE_SIZE
    wb = cache_win.shape[1]
    q, kv, gates = nsa_project(x, w_in, b_gate)
    paged = lambda cache: cache[page_table].reshape((B, past) + cache.shape[2:])
    rows_cmp = jnp.concatenate([paged(cache_cmp), kv[:, :, 0]], axis=1)
    rows_slc = jnp.concatenate([paged(cache_slc), kv[:, :, 1]], axis=1)
    rows_win = jnp.concatenate([cache_win, kv[:, :, 2]], axis=1)
    kc, vc, c_end, ks_blk, vs_blk, cmap = nsa_summaries(rows_cmp, rows_slc, w_cmp, pe_cmp)
    q_pos = past + jnp.arange(T)
    w_pos = jnp.concatenate([past - wb + jnp.arange(wb), q_pos])
    o = nsa_branches(q, gates, q_pos, kc, vc, c_end, ks_blk, vs_blk, rows_win[:, :, 0], rows_win[:, :, 1],
                     w_pos, cmap)
    return o.reshape(B, T, Q_WIDTH) @ w_out, kv


def gmlp_mixer(x, w_in, b_in, ln_v, w_sp, b_sp, w_out):
    B, T, _ = x.shape
    u, v = jnp.split(jax.nn.gelu(x @ w_in + b_in, approximate=False), 2, axis=-1)
    v = layer_norm(v, ln_v[0], ln_v[1])
    c = min(T, GMLP_CHUNK)
    nc = T // c
    w = jnp.tril(w_sp[:, :c, :c])
    vh = v.reshape(B, nc, c, GMLP_GROUPS, GMLP_GROUP_DIM)
    mixed = jnp.einsum('hts,bcshd->bcthd', w, vh) + b_sp[:, :c].T[:, :, None]
    return (u * mixed.reshape(B, T, GMLP_WIDTH)) @ w_out, v


def hgrn2_mixer(x, s0, lb, w_in, norm_gain, w_out):
    B, T, _ = x.shape
    q, f, i, g = jnp.split(x @ w_in, 4, axis=-1)
    heads = lambda a: a.reshape(B, T, HGRN_HEADS, -1).astype(jnp.float32)
    lb = lb.reshape(HGRN_HEADS, HGRN_DK)
    log_f = jnp.logaddexp(jnp.log(lb), jnp.log1p(-lb) + jax.nn.log_sigmoid(heads(f)))
    k = -jnp.expm1(log_f)
    c = HGRN_CHUNK if T % HGRN_CHUNK == 0 else T
    nc = T // c
    chunks = lambda a: a.reshape(B, nc, c, HGRN_HEADS, -1).transpose(1, 0, 3, 2, 4)
    causal = jnp.tril(jnp.ones((c, c), dtype=bool))[:, :, None]

    def step(S, inp):
        qc, kc, vc, lfc = inp
        cum = jnp.cumsum(lfc, axis=2)
        o_inter = jnp.einsum('bhtk,bhkv->bhtv', qc * jnp.exp(cum), S)
        decay = jnp.exp(jnp.where(causal, cum[:, :, :, None, :] - cum[:, :, None, :, :], -jnp.inf))
        attn = jnp.einsum('bhtk,bhsk,bhtsk->bhts', qc, kc, decay)
        o_intra = jnp.einsum('bhts,bhsv->bhtv', attn, vc)
        last = cum[:, :, -1:, :]
        S = jnp.exp(last[:, :, 0, :, None]) * S + jnp.einsum('bhsk,bhsv->bhkv', kc * jnp.exp(last - cum), vc)
        return S, o_inter + o_intra

    S, o = lax.scan(step, s0.astype(jnp.float32), (chunks(heads(q)), chunks(k), chunks(heads(i)), chunks(log_f)))
    o = o.transpose(1, 0, 3, 2, 4).reshape(B, T, HGRN_HEADS, HGRN_DV)
    o = rms_norm(o, norm_gain).reshape(B, T, HGRN_WIDTH).astype(x.dtype)
    return (o * jax.nn.silu(g)) @ w_out, S.astype(s0.dtype)


def setup_inputs(seed: int = 0) -> dict:
    key = jax.random.key(seed)
    ks = iter(jax.random.split(key, 40))
    nrm = lambda shape, scale: jax.random.normal(next(ks), shape, jnp.float32) * scale
    n_pages = PAST_LEN // PAGE_SIZE
    n_phys = (5 * DEC_BATCH * n_pages + 3) // 4
    win_buf = min(WINDOW, PAST_LEN)
    beta = DEEPNORM_BETA
    page_table = jax.random.permutation(next(ks), n_phys)[:DEC_BATCH * n_pages]
    page_table = page_table.reshape(DEC_BATCH, n_pages).astype(jnp.int32)
    return {
        'x_prompt': nrm((BATCH, SEQ, D_MODEL), 1.0),
        'x_sample': nrm((DEC_BATCH, DEC_SEQ, D_MODEL), 1.0),
        'cache_cmp_kv': nrm((N_A, n_phys, PAGE_SIZE, 2, NSA_KV_HEADS, HEAD_DIM), 1.0),
        'cache_slc_kv': nrm((N_A, n_phys, PAGE_SIZE, 2, NSA_KV_HEADS, HEAD_DIM), 1.0),
        'cache_win_kv': nrm((N_A, DEC_BATCH, win_buf, 2, NSA_KV_HEADS, HEAD_DIM), 1.0),
        'state_hgrn': nrm((N_C, DEC_BATCH, HGRN_HEADS, HGRN_DK, HGRN_DV), 0.5),
        'page_table': page_table,
        'ln_gain': 1.0 + nrm((DEPTH, 2, D_MODEL), 0.02),
        'ln_bias': nrm((DEPTH, 2, D_MODEL), 0.02),
        'ffn_w_in': nrm((DEPTH, D_MODEL, 2 * FFN_HIDDEN), D_MODEL ** -0.5),
        'ffn_w_out': nrm((DEPTH, FFN_HIDDEN, D_MODEL), beta * FFN_HIDDEN ** -0.5),
        'nsa_w_in': nrm((N_A, D_MODEL, NSA_IN), D_MODEL ** -0.5),
        'nsa_b_gate': nrm((N_A, 3 * NSA_HEADS), 0.02),
        'nsa_w_cmp': nrm((N_A, 2, CMP_LEN, HEAD_DIM, HEAD_DIM), (CMP_LEN * HEAD_DIM) ** -0.5),
        'nsa_pe_cmp': nrm((N_A, 2, CMP_LEN, HEAD_DIM), 0.1),
        'nsa_w_out': nrm((N_A, Q_WIDTH, D_MODEL), beta * Q_WIDTH ** -0.5),
        'gmlp_w_in': nrm((N_B, D_MODEL, 2 * GMLP_WIDTH), D_MODEL ** -0.5),
        'gmlp_b_in': nrm((N_B, 2 * GMLP_WIDTH), 0.02),
        'gmlp_ln_v': jnp.stack([1.0 + nrm((N_B, GMLP_WIDTH), 0.02), nrm((N_B, GMLP_WIDTH), 0.02)], axis=1),
        'gmlp_w_sp': nrm((N_B, GMLP_GROUPS, GMLP_CHUNK, GMLP_CHUNK), GMLP_CHUNK ** -0.5),
        'gmlp_b_sp': 1.0 + nrm((N_B, GMLP_GROUPS, GMLP_CHUNK), 0.02),
        'gmlp_w_out': nrm((N_B, GMLP_WIDTH, D_MODEL), beta * GMLP_WIDTH ** -0.5),
        'hgrn_w_in': nrm((N_C, D_MODEL, 4 * HGRN_WIDTH), D_MODEL ** -0.5),
        'hgrn_lb_logits': nrm((DEPTH, HGRN_WIDTH), 0.5),
        'hgrn_norm_gain': 1.0 + nrm((N_C, HGRN_DV), 0.02),
        'hgrn_w_out': nrm((N_C, HGRN_WIDTH, D_MODEL), beta * HGRN_WIDTH ** -0.5),
    }


def reference(x_prompt, x_sample, cache_cmp_kv, cache_slc_kv, cache_win_kv, state_hgrn, page_table,
              ln_gain, ln_bias, ffn_w_in, ffn_w_out,
              nsa_w_in, nsa_b_gate, nsa_w_cmp, nsa_pe_cmp, nsa_w_out,
              gmlp_w_in, gmlp_b_in, gmlp_ln_v, gmlp_w_sp, gmlp_b_sp, gmlp_w_out,
              hgrn_w_in, hgrn_lb_logits, hgrn_norm_gain, hgrn_w_out):
    lb_sm = jax.nn.softmax(hgrn_lb_logits.astype(jnp.float32), axis=0)
    lower_bounds = jnp.cumsum(lb_sm, axis=0) - lb_sm[0]
    xp, xs = x_prompt, x_sample
    B, T, _ = xp.shape
    wb_prompt = min(WINDOW, T)
    cmp_p, cmp_s, slc_p, slc_s, win_p, win_s, gv_s, hs_p, hs_s = [], [], [], [], [], [], [], [], []
    for layer in range(DEPTH):
        kind, j = layer % N_MIXERS, layer // N_MIXERS
        if kind == 0:
            hp, kvp = nsa_prompt(xp, nsa_w_in[j], nsa_b_gate[j], nsa_w_cmp[j], nsa_pe_cmp[j], nsa_w_out[j])
            hs, kvs = nsa_sample(xs, cache_cmp_kv[j], cache_slc_kv[j], cache_win_kv[j], page_table,
                                 nsa_w_in[j], nsa_b_gate[j], nsa_w_cmp[j], nsa_pe_cmp[j], nsa_w_out[j])
            page_shape = (B * T // PAGE_SIZE, PAGE_SIZE) + kvp.shape[3:]
            cmp_p.append(kvp[:, :, 0].reshape(page_shape))
            slc_p.append(kvp[:, :, 1].reshape(page_shape))
            win_p.append(kvp[:, T - wb_prompt:, 2])
            cmp_s.append(kvs[:, :, 0])
            slc_s.append(kvs[:, :, 1])
            win_s.append(kvs[:, :, 2])
        elif kind == 1:
            hp, _ = gmlp_mixer(xp, gmlp_w_in[j], gmlp_b_in[j], gmlp_ln_v[j], gmlp_w_sp[j], gmlp_b_sp[j], gmlp_w_out[j])
            hs, v_new = gmlp_mixer(xs, gmlp_w_in[j], gmlp_b_in[j], gmlp_ln_v[j], gmlp_w_sp[j], gmlp_b_sp[j], gmlp_w_out[j])
            gv_s.append(v_new)
        else:
            s_zero = jnp.zeros((B, HGRN_HEADS, HGRN_DK, HGRN_DV), xp.dtype)
            hp, sp = hgrn2_mixer(xp, s_zero, lower_bounds[layer], hgrn_w_in[j], hgrn_norm_gain[j], hgrn_w_out[j])
            hs, ss = hgrn2_mixer(xs, state_hgrn[j], lower_bounds[layer], hgrn_w_in[j], hgrn_norm_gain[j], hgrn_w_out[j])
            hs_p.append(sp)
            hs_s.append(ss)
        xp = deepnorm(xp, hp, ln_gain[layer, 0], ln_bias[layer, 0])
        xs = deepnorm(xs, hs, ln_gain[layer, 0], ln_bias[layer, 0])
        xp = deepnorm(xp, swiglu_ffn(xp, ffn_w_in[layer], ffn_w_out[layer]), ln_gain[layer, 1], ln_bias[layer, 1])
        xs = deepnorm(xs, swiglu_ffn(xs, ffn_w_in[layer], ffn_w_out[layer]), ln_gain[layer, 1], ln_bias[layer, 1])
    y_prompt, y_sample = xp, xs
    new_cmp_kv_prompt = jnp.stack(cmp_p)
    new_cmp_kv_sample = jnp.stack(cmp_s)
    new_slc_kv_prompt = jnp.stack(slc_p)
    new_slc_kv_sample = jnp.stack(slc_s)
    new_win_kv_prompt = jnp.stack(win_p)
    new_win_kv_sample = jnp.stack(win_s)
    new_gmlp_v_sample = jnp.stack(gv_s)
    new_hgrn_prompt = jnp.stack(hs_p)
    new_hgrn_sample = jnp.stack(hs_s)
    return (y_prompt, y_sample, new_cmp_kv_prompt, new_cmp_kv_sample, new_slc_kv_prompt, new_slc_kv_sample,
            new_win_kv_prompt, new_win_kv_sample, new_gmlp_v_sample, new_hgrn_prompt, new_hgrn_sample)
```

```python
import functools

import jax
import jax.numpy as jnp
from jax import lax
from jax.experimental import pallas as pl
from jax.experimental.pallas import tpu as pltpu

F32 = jnp.float32
BF16 = jnp.bfloat16

D_MODEL = 1024
DEPTH = 4
PAGE_SIZE = 128
N_MIXERS = 3

NSA_HEADS = 16
NSA_KV_HEADS = 4
NSA_GROUP = NSA_HEADS // NSA_KV_HEADS
HEAD_DIM = D_MODEL // NSA_HEADS
Q_WIDTH = NSA_HEADS * HEAD_DIM
KV_WIDTH = NSA_KV_HEADS * HEAD_DIM
CMP_LEN = 32
CMP_STRIDE = 16
SLC_BLOCK = 64
TOP_N = 16
WINDOW = 512
FORCE_BONUS = 1e3
ATTN_SCALE = HEAD_DIM ** -0.5
NEG_INF = -1e30

GMLP_WIDTH = D_MODEL
GMLP_GROUPS = 8
GMLP_GROUP_DIM = GMLP_WIDTH // GMLP_GROUPS
GMLP_CHUNK = 128

HGRN_HEADS = 8
HGRN_DK = 128
HGRN_DV = 128
HGRN_WIDTH = HGRN_HEADS * HGRN_DK

FFN_HIDDEN = 2816
DEEPNORM_ALPHA = (2 * DEPTH) ** 0.25
LN_EPS = 1e-5

LANES = 128
VMEM_LIMIT = 48 * 1024 * 1024


def _params(*sem):
    return pltpu.CompilerParams(dimension_semantics=sem, vmem_limit_bytes=VMEM_LIMIT)


def _dot(a, b):
    return jnp.dot(a, b, preferred_element_type=F32)


def _dot_nt(a, b):
    return lax.dot_general(a, b, (((1,), (1,)), ((), ())), preferred_element_type=F32)


def _layer_norm_rows(y, g, b):
    mu = jnp.mean(y, axis=-1, keepdims=True)
    yc = y - mu
    var = jnp.mean(yc * yc, axis=-1, keepdims=True)
    return yc * lax.rsqrt(var + LN_EPS) * g + b


def _deepnorm_rows(x, h, g, b):
    return _layer_norm_rows(DEEPNORM_ALPHA * x + h, g, b)


def _split3(x):
    hi = x.astype(BF16)
    r1 = x - hi.astype(F32)
    mid = r1.astype(BF16)
    lo = (r1 - mid.astype(F32)).astype(BF16)
    return hi, mid, lo


def _proj_deepnorm_kernel(a_ref, w_ref, x_ref, g_ref, b_ref, o_ref):
    h = _dot(a_ref[...].astype(BF16), w_ref[...])
    o_ref[...] = _deepnorm_rows(x_ref[...], h, g_ref[...], b_ref[...])


def proj_deepnorm(a, w_bf, x, g, b, tm):
    m, k = a.shape
    d = x.shape[1]
    tm = min(tm, m)
    return pl.pallas_call(
        _proj_deepnorm_kernel,
        grid=(m // tm,),
        in_specs=[pl.BlockSpec((tm, k), lambda i: (i, 0)),
                  pl.BlockSpec((k, d), lambda i: (0, 0)),
                  pl.BlockSpec((tm, d), lambda i: (i, 0)),
                  pl.BlockSpec((1, d), lambda i: (0, 0)),
                  pl.BlockSpec((1, d), lambda i: (0, 0))],
        out_specs=pl.BlockSpec((tm, d), lambda i: (i, 0)),
        out_shape=jax.ShapeDtypeStruct((m, d), F32),
        compiler_params=_params("parallel"),
        name="proj_deepnorm",
    )(a, w_bf, x, g.reshape(1, d), b.reshape(1, d))


def _ffn_kernel(x_ref, wg_ref, wu_ref, wo_ref, g_ref, b_ref, o_ref, xb_ref, acc_ref):
    j = pl.program_id(1)

    @pl.when(j == 0)
    def _():
        xb_ref[...] = x_ref[...].astype(BF16)
        acc_ref[...] = jnp.zeros_like(acc_ref)

    xb = xb_ref[...]
    gate = _dot(xb, wg_ref[...])
    up = _dot(xb, wu_ref[...])
    mid = (gate * jax.nn.sigmoid(gate) * up).astype(BF16)
    acc_ref[...] += _dot(mid, wo_ref[...])

    @pl.when(j == pl.num_programs(1) - 1)
    def _():
        o_ref[...] = _deepnorm_rows(x_ref[...], acc_ref[...], g_ref[...], b_ref[...])


def ffn_deepnorm(x, w_in_bf, w_out_bf, g, b, tm, hc):
    m, d = x.shape
    hidden = w_out_bf.shape[0]
    tm = min(tm, m)
    nh = hidden // hc
    return pl.pallas_call(
        _ffn_kernel,
        grid=(m // tm, nh),
        in_specs=[pl.BlockSpec((tm, d), lambda i, j: (i, 0)),
                  pl.BlockSpec((d, hc), lambda i, j: (0, j)),
                  pl.BlockSpec((d, hc), lambda i, j: (0, nh + j)),
                  pl.BlockSpec((hc, d), lambda i, j: (j, 0)),
                  pl.BlockSpec((1, d), lambda i, j: (0, 0)),
                  pl.BlockSpec((1, d), lambda i, j: (0, 0))],
        out_specs=pl.BlockSpec((tm, d), lambda i, j: (i, 0)),
        out_shape=jax.ShapeDtypeStruct((m, d), F32),
        scratch_shapes=[pltpu.VMEM((tm, d), BF16), pltpu.VMEM((tm, d), F32)],
        compiler_params=_params("parallel", "arbitrary"),
        name="ffn_deepnorm",
    )(x, w_in_bf, w_in_bf, w_out_bf, g.reshape(1, d), b.reshape(1, d))


def _gelu_exact(z):
    return 0.5 * z * (1.0 + lax.erf(z * (0.5 ** 0.5)))


def _gmlp_kernel(x_ref, win_ref, bin_ref, lnv_ref, wsp_ref, bsp_ref, wout_ref, g_ref, b_ref,
                 o_ref, *rest, chunked):
    width = GMLP_WIDTH
    x = x_ref[...]
    z = _dot(x.astype(BF16), win_ref[...]) + bin_ref[...]
    gz = _gelu_exact(z)
    u = gz[:, :width]
    v = _layer_norm_rows(gz[:, width:], lnv_ref[0:1, :], lnv_ref[1:2, :])
    if chunked:
        mix_ref, = rest
        vb = v.astype(BF16)
        for c in range(x.shape[0] // GMLP_CHUNK):
            rows = slice(c * GMLP_CHUNK, (c + 1) * GMLP_CHUNK)
            for h in range(GMLP_GROUPS):
                cols = slice(h * GMLP_GROUP_DIM, (h + 1) * GMLP_GROUP_DIM)
                mix_ref[rows, cols] = _dot(wsp_ref[h], vb[rows, cols]) + bsp_ref[:, cols]
        mixed = mix_ref[...]
    else:
        v_ref, = rest
        v_ref[...] = v
        mixed = v * wsp_ref[...] + bsp_ref[...]
    h_out = _dot((u * mixed).astype(BF16), wout_ref[...])
    o_ref[...] = _deepnorm_rows(x, h_out, g_ref[...], b_ref[...])


def gmlp_layer(x, w_in_bf, b_in, ln_v, w_sp, b_sp, w_out_bf, g, b, *, chunked, tm):
    m, d = x.shape
    width = GMLP_WIDTH
    tm = min(tm, m)
    const = lambda *shape: pl.BlockSpec(shape, lambda i: (0,) * len(shape))
    if chunked:
        wsp = jnp.tril(w_sp).astype(BF16)
        bsp = jnp.repeat(b_sp.T, GMLP_GROUP_DIM, axis=1)
        wsp_spec, bsp_spec = const(GMLP_GROUPS, GMLP_CHUNK, GMLP_CHUNK), const(GMLP_CHUNK, width)
        out_shape = jax.ShapeDtypeStruct((m, d), F32)
        out_specs = pl.BlockSpec((tm, d), lambda i: (i, 0))
        scratch = [pltpu.VMEM((tm, width), F32)]
    else:
        wsp = jnp.repeat(w_sp[:, 0, 0], GMLP_GROUP_DIM).reshape(1, width)
        bsp = jnp.repeat(b_sp[:, 0], GMLP_GROUP_DIM).reshape(1, width)
        wsp_spec, bsp_spec = const(1, width), const(1, width)
        out_shape = (jax.ShapeDtypeStruct((m, d), F32), jax.ShapeDtypeStruct((m, width), F32))
        out_specs = (pl.BlockSpec((tm, d), lambda i: (i, 0)), pl.BlockSpec((tm, width), lambda i: (i, 0)))
        scratch = []
    return pl.pallas_call(
        functools.partial(_gmlp_kernel, chunked=chunked),
        grid=(m // tm,),
        in_specs=[pl.BlockSpec((tm, d), lambda i: (i, 0)),
                  const(d, 2 * width), const(1, 2 * width), const(2, width),
                  wsp_spec, bsp_spec, const(width, d), const(1, d), const(1, d)],
        out_specs=out_specs,
        out_shape=out_shape,
        scratch_shapes=scratch,
        compiler_params=_params("parallel"),
        name="gmlp_layer",
    )(x, w_in_bf, b_in.reshape(1, 2 * width), ln_v, wsp, bsp, w_out_bf, g.reshape(1, d), b.reshape(1, d))


def _linear_kernel(x_ref, w_ref, o_ref):
    o_ref[...] = _dot(x_ref[...].astype(BF16), w_ref[...])


def linear(x, w_bf, tm, tn):
    m, k = x.shape
    n = w_bf.shape[1]
    tm, tn = min(tm, m), min(tn, n)
    return pl.pallas_call(
        _linear_kernel,
        grid=(m // tm, n // tn),
        in_specs=[pl.BlockSpec((tm, k), lambda i, j: (i, 0)),
                  pl.BlockSpec((k, tn), lambda i, j: (0, j))],
        out_specs=pl.BlockSpec((tm, tn), lambda i, j: (i, j)),
        out_shape=jax.ShapeDtypeStruct((m, n), F32),
        compiler_params=_params("parallel", "parallel"),
        name="linear",
    )(x, w_bf)


NSA_QB = 128
NSA_GATE_ROWS = 64
NSA_T_ROWS = Q_WIDTH + 2 * KV_WIDTH + NSA_GATE_ROWS


def _nsa_proj_kernel(x_ref, wkv_ref, wt_ref, bg_ref,
                     kvc_ref, kvs_ref, kvw_ref, ks_ref, kw_ref, qt_ref, vst_ref, vwt_ref, gt_ref):
    xb = x_ref[0].astype(BF16)
    kv = _dot(xb, wkv_ref[...])
    w2 = 2 * KV_WIDTH
    kvc_ref[0] = kv[:, 0:w2]
    kvs_ref[0] = kv[:, w2:2 * w2]
    kvw_ref[0] = kv[:, 2 * w2:3 * w2]
    for g in range(NSA_KV_HEADS):
        ks_ref[0, g] = kv[:, w2 + g * HEAD_DIM:w2 + (g + 1) * HEAD_DIM].astype(BF16)
        kw_ref[0, g] = kv[:, 2 * w2 + g * HEAD_DIM:2 * w2 + (g + 1) * HEAD_DIM].astype(BF16)
    t = _dot_nt(wt_ref[...], xb)
    for g in range(NSA_KV_HEADS):
        r0 = Q_WIDTH + g * HEAD_DIM
        vst_ref[0, g] = t[r0:r0 + HEAD_DIM, :].astype(BF16)
        vwt_ref[0, g] = t[r0 + KV_WIDTH:r0 + KV_WIDTH + HEAD_DIM, :].astype(BF16)
    gates = jax.nn.sigmoid(t[Q_WIDTH + 2 * KV_WIDTH:, :] + bg_ref[...])
    for nb in range(xb.shape[0] // NSA_QB):
        cols = slice(nb * NSA_QB, (nb + 1) * NSA_QB)
        gt_ref[0, nb] = gates[:, cols]
        for g in range(NSA_KV_HEADS):
            for r in range(NSA_GROUP):
                r0 = (g * NSA_GROUP + r) * HEAD_DIM
                qt_ref[0, nb, g, :, r * NSA_QB:(r + 1) * NSA_QB] = (
                    t[r0:r0 + HEAD_DIM, cols] * ATTN_SCALE).astype(BF16)


def nsa_prompt_proj(x, w_in, b_gate, tq):
    bsz, t, d = x.shape
    nb, nbk = t // NSA_QB, tq // NSA_QB
    g_, hd, w2 = NSA_KV_HEADS, HEAD_DIM, 2 * KV_WIDTH
    w_q = w_in[:, :Q_WIDTH]
    w_kv = w_in[:, Q_WIDTH:Q_WIDTH + 6 * KV_WIDTH]
    w_g = w_in[:, Q_WIDTH + 6 * KV_WIDTH:]
    v_slc = w_kv[:, w2 + KV_WIDTH:2 * w2]
    v_win = w_kv[:, 2 * w2 + KV_WIDTH:3 * w2]
    pad = jnp.zeros((d, NSA_GATE_ROWS - 3 * NSA_HEADS), w_in.dtype)
    w_t = jnp.concatenate([w_q, v_slc, v_win, w_g, pad], axis=1).T.astype(BF16)
    bg = jnp.concatenate([b_gate, jnp.zeros((NSA_GATE_ROWS - 3 * NSA_HEADS,), b_gate.dtype)]).reshape(NSA_GATE_ROWS, 1)
    row = lambda width: pl.BlockSpec((1, tq, width), lambda b, i: (b, i, 0))
    out_shape = (
        jax.ShapeDtypeStruct((bsz, t, w2), F32), jax.ShapeDtypeStruct((bsz, t, w2), F32),
        jax.ShapeDtypeStruct((bsz, t, w2), F32),
        jax.ShapeDtypeStruct((bsz, g_, t, hd), BF16), jax.ShapeDtypeStruct((bsz, g_, t, hd), BF16),
        jax.ShapeDtypeStruct((bsz, nb, g_, hd, NSA_GROUP * NSA_QB), BF16),
        jax.ShapeDtypeStruct((bsz, g_, hd, t), BF16), jax.ShapeDtypeStruct((bsz, g_, hd, t), BF16),
        jax.ShapeDtypeStruct((bsz, nb, NSA_GATE_ROWS, NSA_QB), F32),
    )
    out_specs = (
        row(w2), row(w2), row(w2),
        pl.BlockSpec((1, g_, tq, hd), lambda b, i: (b, 0, i, 0)),
        pl.BlockSpec((1, g_, tq, hd), lambda b, i: (b, 0, i, 0)),
        pl.BlockSpec((1, nbk, g_, hd, NSA_GROUP * NSA_QB), lambda b, i: (b, i, 0, 0, 0)),
        pl.BlockSpec((1, g_, hd, tq), lambda b, i: (b, 0, 0, i)),
        pl.BlockSpec((1, g_, hd, tq), lambda b, i: (b, 0, 0, i)),
        pl.BlockSpec((1, nbk, NSA_GATE_ROWS, NSA_QB), lambda b, i: (b, i, 0, 0)),
    )
    return pl.pallas_call(
        _nsa_proj_kernel,
        grid=(bsz, t // tq),
        in_specs=[pl.BlockSpec((1, tq, d), lambda b, i: (b, i, 0)),
                  pl.BlockSpec((d, 6 * KV_WIDTH), lambda b, i: (0, 0)),
                  pl.BlockSpec((NSA_T_ROWS, d), lambda b, i: (0, 0)),
                  pl.BlockSpec((NSA_GATE_ROWS, 1), lambda b, i: (0, 0))],
        out_specs=out_specs,
        out_shape=out_shape,
        compiler_params=_params("parallel", "parallel"),
        name="nsa_prompt_proj",
    )(x, w_kv.astype(BF16), w_t, bg)


def _cmp_weights(w_cmp):
    w = w_cmp.reshape(2, 2, CMP_STRIDE, HEAD_DIM, HEAD_DIM)
    eye = jnp.eye(NSA_KV_HEADS, dtype=w_cmp.dtype)
    wbd = jnp.einsum('gh,klpde->pkgdlhe', eye, w)
    return wbd.reshape(CMP_STRIDE, 2, KV_WIDTH, 2 * KV_WIDTH).astype(BF16)


def _cmp_bias(pe_ref, wflat_ref, kv):
    pe = jnp.broadcast_to(pe_ref[kv:kv + 1, :], (8, CMP_LEN * HEAD_DIM)).astype(BF16)
    bias = _dot(pe, wflat_ref[kv])[0:1, :]
    return jnp.concatenate([bias] * NSA_KV_HEADS, axis=1)


def _nsa_compress_kernel(*refs, n_pages, n_cmp, transpose_v, paged):
    if paged:
        refs = refs[1:]
    x_refs = refs[:n_pages]
    wk_ref, wv_ref, pe_ref, wflat_ref, wvt_ref, kc_ref, vc_ref = refs[n_pages:]
    if n_pages == 1:
        x = x_refs[0][0]
    else:
        x = jnp.concatenate([r[0] for r in x_refs], axis=0)
    xb = x.astype(BF16)
    nhb = xb.shape[0]
    kw = KV_WIDTH
    acc_k = jnp.zeros((nhb, 2 * kw), F32)
    acc_v = jnp.zeros((2 * kw, nhb), F32) if transpose_v else jnp.zeros((nhb, 2 * kw), F32)
    for p in range(CMP_STRIDE):
        xk = xb[:, p * 2 * kw:p * 2 * kw + kw]
        xv = xb[:, p * 2 * kw + kw:(p + 1) * 2 * kw]
        acc_k = acc_k + _dot(xk, wk_ref[p])
        acc_v = acc_v + (_dot_nt(wv_ref[p], xv) if transpose_v else _dot(xv, wv_ref[p]))
    k_all = acc_k[:, :kw] + pltpu.roll(acc_k[:, kw:], nhb - 1, 0) + _cmp_bias(pe_ref, wflat_ref, 0)
    valid_rows = lax.broadcasted_iota(jnp.int32, (nhb, 1), 0) < n_cmp
    k_all = jnp.where(valid_rows, k_all, 0.0)
    if transpose_v:
        pe8 = jnp.broadcast_to(pe_ref[1:2, :], (8, CMP_LEN * HEAD_DIM)).astype(BF16)
        bias_col = _dot_nt(wvt_ref[...], pe8)[:, 0:1]
        bias_col = jnp.concatenate([bias_col] * NSA_KV_HEADS, axis=0)
        v_t = acc_v[:kw, :] + pltpu.roll(acc_v[kw:, :], nhb - 1, 1) + bias_col
        valid_cols = lax.broadcasted_iota(jnp.int32, (1, nhb), 1) < n_cmp
        v_t = jnp.where(valid_cols, v_t, 0.0)
        for g in range(NSA_KV_HEADS):
            kc_ref[0, g] = k_all[:, g * HEAD_DIM:(g + 1) * HEAD_DIM].astype(BF16)
            vc_ref[0, g] = v_t[g * HEAD_DIM:(g + 1) * HEAD_DIM, :].astype(BF16)
    else:
        v_all = acc_v[:, :kw] + pltpu.roll(acc_v[:, kw:], nhb - 1, 0) + _cmp_bias(pe_ref, wflat_ref, 1)
        kc_ref[0] = k_all.astype(BF16)
        vc_ref[0] = jnp.where(valid_rows, v_all, 0.0).astype(BF16)


def _cmp_operands(w_cmp, pe_cmp, transpose_v):
    wbd = _cmp_weights(w_cmp)
    wk = wbd[:, 0]
    wv = jnp.swapaxes(wbd[:, 1], 1, 2) if transpose_v else wbd[:, 1]
    pe = pe_cmp.reshape(2, CMP_LEN * HEAD_DIM)
    wflat = w_cmp.reshape(2, CMP_LEN * HEAD_DIM, HEAD_DIM).astype(BF16)
    wvt = wflat[1].T
    const = lambda a: pl.BlockSpec(a.shape, lambda *_: (0,) * a.ndim)
    ops = (wk, wv, pe, wflat, wvt)
    return ops, [const(a) for a in ops]


def nsa_compress_prompt(kv_cmp, w_cmp, pe_cmp):
    bsz, t, w2 = kv_cmp.shape
    nhb = t // CMP_STRIDE
    n_cmp = (t - CMP_LEN) // CMP_STRIDE + 1
    ops, specs = _cmp_operands(w_cmp, pe_cmp, True)
    x = kv_cmp.reshape(bsz, nhb, CMP_STRIDE * w2)
    return pl.pallas_call(
        functools.partial(_nsa_compress_kernel, n_pages=1, n_cmp=n_cmp, transpose_v=True, paged=False),
        grid=(bsz,),
        in_specs=[pl.BlockSpec((1, nhb, CMP_STRIDE * w2), lambda b: (b, 0, 0))] + specs,
        out_specs=(pl.BlockSpec((1, NSA_KV_HEADS, nhb, HEAD_DIM), lambda b: (b, 0, 0, 0)),
                   pl.BlockSpec((1, NSA_KV_HEADS, HEAD_DIM, nhb), lambda b: (b, 0, 0, 0))),
        out_shape=(jax.ShapeDtypeStruct((bsz, NSA_KV_HEADS, nhb, HEAD_DIM), BF16),
                   jax.ShapeDtypeStruct((bsz, NSA_KV_HEADS, HEAD_DIM, nhb), BF16)),
        compiler_params=_params("parallel"),
        name="nsa_compress_prompt",
    )(x, *ops)


def nsa_compress_sample(cache_cmp, page_table, w_cmp, pe_cmp):
    bsz, n_pages = page_table.shape
    w2 = 2 * KV_WIDTH
    hb_page = PAGE_SIZE // CMP_STRIDE
    nhb = n_pages * hb_page
    n_cmp = (n_pages * PAGE_SIZE - CMP_LEN) // CMP_STRIDE + 1
    ops, specs = _cmp_operands(w_cmp, pe_cmp, False)
    pages = cache_cmp.reshape(cache_cmp.shape[0], hb_page, CMP_STRIDE * w2)
    page_specs = [pl.BlockSpec((1, hb_page, CMP_STRIDE * w2), lambda b, pt, k=k: (pt[b, k], 0, 0))
                  for k in range(n_pages)]
    out = jax.ShapeDtypeStruct((bsz, nhb, KV_WIDTH), BF16)
    out_spec = pl.BlockSpec((1, nhb, KV_WIDTH), lambda b, pt: (b, 0, 0))
    return pl.pallas_call(
        functools.partial(_nsa_compress_kernel, n_pages=n_pages, n_cmp=n_cmp, transpose_v=False, paged=True),
        grid_spec=pltpu.PrefetchScalarGridSpec(
            num_scalar_prefetch=1, grid=(bsz,),
            in_specs=page_specs + specs, out_specs=(out_spec, out_spec)),
        out_shape=(out, out),
        compiler_params=_params("parallel"),
        name="nsa_compress_sample",
    )(page_table, *([pages] * n_pages), *ops)


def _cmap_t(n_cmp, n_slc, rows, cols):
    c0 = jnp.arange(cols)[None, :] * CMP_STRIDE
    s0 = jnp.arange(rows)[:, None] * SLC_BLOCK
    ov = jnp.minimum(c0 + CMP_LEN, s0 + SLC_BLOCK) - jnp.maximum(c0, s0)
    ov = jnp.clip(ov, 0, None).astype(F32) / CMP_STRIDE
    keep = (jnp.arange(cols)[None, :] < n_cmp) & (jnp.arange(rows)[:, None] < n_slc)
    return jnp.where(keep, ov, 0.0).astype(BF16)


def _softmax_cols(s, mask):
    sm = jnp.where(mask, s, NEG_INF)
    m = jnp.max(sm, axis=0, keepdims=True)
    e = jnp.where(mask, jnp.exp(sm - m), 0.0)
    l = jnp.sum(e, axis=0, keepdims=True)
    return e / jnp.where(l > 0.0, l, 1.0)


def _lanes_x4(a):
    return jnp.concatenate([a] * NSA_GROUP, axis=1)


def _nsa_prompt_attn_kernel(qt_ref, gt_ref, kc_ref, vct_ref, ks_ref, vst_ref, kw_ref, vwt_ref, cmap_ref,
                            o_ref, sel_ref, *, n_cmp, n_slc):
    qb = NSA_QB
    g = pl.program_id(1)
    n = pl.program_id(2)
    s0 = n * qb
    qt = qt_ref[0, 0, 0]
    qpos1 = s0 + lax.broadcasted_iota(jnp.int32, (1, qb), 1)
    qpos = _lanes_x4(qpos1)
    row = lax.broadcasted_iota(jnp.int32, (qb, 1), 0)

    ncp = kc_ref.shape[2]
    crow = lax.broadcasted_iota(jnp.int32, (ncp, 1), 0)
    cmask = (crow * CMP_STRIDE + (CMP_LEN - 1) <= qpos) & (crow < n_cmp)
    p_cmp = _softmax_cols(_dot(kc_ref[0, 0], qt), cmask)
    o_cmp = _dot(vct_ref[0, 0], p_cmp.astype(BF16))

    psum = p_cmp[:, 0:qb]
    for r in range(1, NSA_GROUP):
        psum = psum + p_cmp[:, r * qb:(r + 1) * qb]
    imp = sum(_dot(cmap_ref[...], piece) for piece in _split3(psum))
    nsp = cmap_ref.shape[0]
    blk = lax.broadcasted_iota(jnp.int32, (nsp, 1), 0)
    cur = qpos1 // SLC_BLOCK
    forced = (blk == 0) | (blk == cur) | (blk == cur - 1)
    allowed = (blk <= cur) & (blk < n_slc)
    score = jnp.where(allowed, imp + jnp.where(forced, FORCE_BONUS, 0.0), -jnp.inf)
    rank = jnp.zeros((nsp, qb), jnp.int32)
    for j in range(n_slc):
        other = score[j:j + 1, :]
        before = (other > score) | ((other == score) & (j < blk))
        rank = rank + before.astype(jnp.int32)
    sel_ref[...] = ((rank < min(TOP_N, n_slc)) & allowed).astype(F32)

    def attend(k_ref, vt_ref, j_lo, j_hi, mask_fn):
        def body(j, carry):
            m, l, acc = carry
            off = pl.multiple_of(j * qb, qb)
            s = _dot(k_ref[0, 0, pl.ds(off, qb), :], qt)
            msk = mask_fn(j, off + row)
            s = jnp.where(msk, s, NEG_INF)
            m_new = jnp.maximum(m, jnp.max(s, axis=0, keepdims=True))
            alpha = jnp.exp(m - m_new)
            e = jnp.where(msk, jnp.exp(s - m_new), 0.0)
            l = alpha * l + jnp.sum(e, axis=0, keepdims=True)
            acc = alpha * acc + _dot(vt_ref[0, 0, :, pl.ds(off, qb)], e.astype(BF16))
            return m_new, l, acc

        width = NSA_GROUP * qb
        init = (jnp.full((1, width), NEG_INF, F32), jnp.zeros((1, width), F32),
                jnp.zeros((HEAD_DIM, width), F32))
        _, l, acc = lax.fori_loop(j_lo, j_hi, body, init)
        return acc / l

    per_tile = qb // SLC_BLOCK

    def slc_mask(j, kpos):
        blk_mask = jnp.zeros((qb, qb), F32)
        for i in range(per_tile):
            sel_row = sel_ref[pl.ds(j * per_tile + i, 1), :]
            in_blk = (row >= i * SLC_BLOCK) & (row < (i + 1) * SLC_BLOCK)
            blk_mask = jnp.where(in_blk, sel_row, blk_mask)
        return (_lanes_x4(blk_mask) > 0.5) & (kpos <= qpos)

    o_slc = attend(ks_ref, vst_ref, 0, n + 1, slc_mask)

    def win_mask(j, kpos):
        return (kpos <= qpos) & (kpos > qpos - WINDOW)

    o_win = attend(kw_ref, vwt_ref, jnp.maximum(n - WINDOW // qb, 0), n + 1, win_mask)

    def gate(branch):
        rows = gt_ref[0, 0, pl.ds(branch * NSA_HEADS + g * NSA_GROUP, NSA_GROUP), :]
        return jnp.concatenate([rows[r:r + 1, :] for r in range(NSA_GROUP)], axis=1)

    o = gate(0) * o_cmp + gate(1) * o_slc + gate(2) * o_win
    for r in range(NSA_GROUP):
        o_ref[0, 0, r * HEAD_DIM:(r + 1) * HEAD_DIM, :] = o[:, r * qb:(r + 1) * qb]


def nsa_prompt_attention(qt, gt, kc, vct, ks, vst, kw, vwt):
    bsz, nb, g_, hd, width = qt.shape
    t = ks.shape[2]
    ncp = kc.shape[2]
    n_cmp = (t - CMP_LEN) // CMP_STRIDE + 1
    n_slc = -(-t // SLC_BLOCK)
    nsp = -(-n_slc // 8) * 8
    cmap = _cmap_t(n_cmp, n_slc, nsp, ncp)
    per_bg = lambda *shape: pl.BlockSpec((1, 1) + shape, lambda b, g, n: (b, g, 0, 0))
    return pl.pallas_call(
        functools.partial(_nsa_prompt_attn_kernel, n_cmp=n_cmp, n_slc=n_slc),
        grid=(bsz, g_, nb),
        in_specs=[pl.BlockSpec((1, 1, 1, hd, width), lambda b, g, n: (b, n, g, 0, 0)),
                  pl.BlockSpec((1, 1, NSA_GATE_ROWS, NSA_QB), lambda b, g, n: (b, n, 0, 0)),
                  per_bg(ncp, hd), per_bg(hd, ncp),
                  per_bg(t, hd), per_bg(hd, t), per_bg(t, hd), per_bg(hd, t),
                  pl.BlockSpec((nsp, ncp), lambda b, g, n: (0, 0))],
        out_specs=pl.BlockSpec((1, 1, NSA_GROUP * hd, NSA_QB), lambda b, g, n: (b, n, g, 0)),
        out_shape=jax.ShapeDtypeStruct((bsz, nb, Q_WIDTH, NSA_QB), F32),
        scratch_shapes=[pltpu.VMEM((nsp, NSA_QB), F32)],
        compiler_params=_params("parallel", "parallel", "arbitrary"),
        name="nsa_prompt_attention",
    )(qt, gt, kc, vct, ks, vst, kw, vwt, cmap)


def _proj_t_deepnorm_kernel(at_ref, w_ref, x_ref, g_ref, b_ref, o_ref):
    a = at_ref[0, 0].T.astype(BF16)
    o_ref[0] = _deepnorm_rows(x_ref[0], _dot(a, w_ref[...]), g_ref[...], b_ref[...])


def proj_t_deepnorm(a_t, w_bf, x, g, b):
    bsz, nb, k, qb = a_t.shape
    d = x.shape[2]
    return pl.pallas_call(
        _proj_t_deepnorm_kernel,
        grid=(bsz, nb),
        in_specs=[pl.BlockSpec((1, 1, k, qb), lambda b, n: (b, n, 0, 0)),
                  pl.BlockSpec((k, d), lambda b, n: (0, 0)),
                  pl.BlockSpec((1, qb, d), lambda b, n: (b, n, 0)),
                  pl.BlockSpec((1, d), lambda b, n: (0, 0)),
                  pl.BlockSpec((1, d), lambda b, n: (0, 0))],
        out_specs=pl.BlockSpec((1, qb, d), lambda b, n: (b, n, 0)),
        out_shape=jax.ShapeDtypeStruct(x.shape, F32),
        compiler_params=_params("parallel", "parallel"),
        name="proj_t_deepnorm",
    )(a_t, w_bf, x, g.reshape(1, d), b.reshape(1, d))


HGRN_C = 128
HGRN_SUB = 8
HGRN_LEVELS = (8, 16, 32, 64)


def _hgrn_lower_bound(logits, layer):
    e = jnp.exp(logits - jnp.max(logits, axis=0, keepdims=True))
    sm = e / jnp.sum(e, axis=0, keepdims=True)
    lb = jnp.zeros_like(sm[0])
    for j in range(1, layer + 1):
        lb = lb + sm[j]
    return lb


def _hgrn_gates(f_raw, lb):
    log_sig = jnp.minimum(f_raw, 0.0) - jnp.log1p(jnp.exp(-jnp.abs(f_raw)))
    a = jnp.log(lb)
    b = jnp.log1p(-lb) + log_sig
    log_f = jnp.maximum(a, b) + jnp.log1p(jnp.exp(-jnp.abs(a - b)))
    return log_f, (1.0 - lb) * (1.0 - jax.nn.sigmoid(f_raw))


def _rms_gate(o, gain, g_raw):
    o = o * lax.rsqrt(jnp.mean(o * o, axis=-1, keepdims=True) + LN_EPS) * gain
    return o * (g_raw * jax.nn.sigmoid(g_raw))


def _hgrn_prompt_kernel(q_ref, f_ref, i_ref, g_ref, lb_ref, gain_ref, tri_ref, lmask_ref,
                        y_ref, s_ref, st_ref, cum_ref, *, layer):
    c = HGRN_C
    nblk = c // HGRN_SUB
    lb = _hgrn_lower_bound(lb_ref[:, 0], layer)
    st_ref[...] = jnp.zeros_like(st_ref)
    rsub = lax.broadcasted_iota(jnp.int32, (c, 1), 0) % HGRN_SUB

    def chunk(ci, _):
        rows = pl.ds(pl.multiple_of(ci * c, c), c)
        q = q_ref[0, rows, :]
        v = i_ref[0, rows, :]
        lf, k = _hgrn_gates(f_ref[0, rows, :], lb)
        cum = sum(_dot(tri_ref[...], piece) for piece in _split3(lf))
        cum_ref[...] = cum
        ends = cum_ref[pl.ds(HGRN_SUB - 1, nblk, stride=HGRN_SUB), :]
        end_b = [jnp.broadcast_to(ends[i:i + 1, :], (HGRN_SUB, HGRN_DK)) for i in range(nblk)]
        zero_b = jnp.zeros((HGRN_SUB, HGRN_DK), F32)
        vb = v.astype(BF16)

        a_off = jnp.zeros((c, c), F32)
        for li, lev in enumerate(HGRN_LEVELS):
            per = lev // HGRN_SUB
            before = [end_b[(i // per) * per - 1] if i >= per else zero_b for i in range(nblk)]
            after = [end_b[(i // per) * per + per - 1] for i in range(nblk)]
            qd = q * jnp.exp(cum - jnp.concatenate(before, axis=0))
            kd = k * jnp.exp(jnp.concatenate(after, axis=0) - cum)
            a_off = a_off + _dot_nt(qd.astype(BF16), kd.astype(BF16)) * lmask_ref[li]
        o = _dot(a_off.astype(BF16), vb)

        for j in range(HGRN_SUB):
            if j == 0:
                w = q * k
                vj = v
            else:
                ok = rsub >= j
                w = jnp.where(ok, q * pltpu.roll(k, j, 0) * jnp.exp(cum - pltpu.roll(cum, j, 0)), 0.0)
                vj = pltpu.roll(v, j, 0)
            o = o + jnp.sum(w, axis=1, keepdims=True) * vj

        last = end_b[nblk - 1][0:1, :]
        o = o + _dot_nt((q * jnp.exp(cum)).astype(BF16), st_ref[...].astype(BF16))
        kd = (k * jnp.exp(last - cum)).astype(BF16)
        st_ref[...] = st_ref[...] * jnp.exp(last) + _dot(v.T.astype(BF16), kd)
        y_ref[0, rows, :] = _rms_gate(o, gain_ref[...], g_ref[0, rows, :])
        return 0

    lax.fori_loop(0, q_ref.shape[1] // c, chunk, 0)
    s_ref[0, 0] = st_ref[...].T


def hgrn_prompt(proj, lb_logits, gain, layer):
    bsz, t, _ = proj.shape
    c = HGRN_C
    hh = HGRN_HEADS
    tri = jnp.tril(jnp.ones((c, c), F32)).astype(BF16)
    ti = jnp.arange(c)[:, None]
    si = jnp.arange(c)[None, :]
    lmask = jnp.stack([((ti // (2 * lev) == si // (2 * lev)) & ((ti // lev) % 2 == 1) & ((si // lev) % 2 == 0))
                       for lev in HGRN_LEVELS]).astype(F32)
    col = lambda part: pl.BlockSpec((1, t, HGRN_DK), lambda b, h: (b, 0, part * hh + h))
    return pl.pallas_call(
        functools.partial(_hgrn_prompt_kernel, layer=layer),
        grid=(bsz, hh),
        in_specs=[col(0), col(1), col(2), col(3),
                  pl.BlockSpec((lb_logits.shape[0], 1, 1, HGRN_DK), lambda b, h: (0, h, 0, 0)),
                  pl.BlockSpec((1, HGRN_DV), lambda b, h: (0, 0)),
                  pl.BlockSpec((c, c), lambda b, h: (0, 0)),
                  pl.BlockSpec((len(HGRN_LEVELS), c, c), lambda b, h: (0, 0, 0))],
        out_specs=(pl.BlockSpec((1, t, HGRN_DV), lambda b, h: (b, 0, h)),
                   pl.BlockSpec((1, 1, HGRN_DK, HGRN_DV), lambda b, h: (b, h, 0, 0))),
        out_shape=(jax.ShapeDtypeStruct((bsz, t, hh * HGRN_DV), F32),
                   jax.ShapeDtypeStruct((bsz, hh, HGRN_DK, HGRN_DV), F32)),
        scratch_shapes=[pltpu.VMEM((HGRN_DV, HGRN_DK), F32), pltpu.VMEM((c, HGRN_DK), F32)],
        compiler_params=_params("parallel", "parallel"),
        name="hgrn_prompt",
    )(proj, proj, proj, proj, lb_logits.reshape(lb_logits.shape[0], hh, 1, HGRN_DK),
      gain.reshape(1, HGRN_DV), tri, lmask)


def _hgrn_sample_kernel(p_ref, s0_ref, lb_ref, gain_ref, y_ref, s_ref, *, layer):
    hh = HGRN_HEADS
    p = p_ref[0]
    q8, f8, i8, g8 = (p[j * hh:(j + 1) * hh, :] for j in range(4))
    lf8, k8 = _hgrn_gates(f8, _hgrn_lower_bound(lb_ref[...], layer))
    pad = jnp.zeros((LANES - 2 * hh, HGRN_DK), F32)
    cols = jnp.concatenate([jnp.exp(lf8), k8, pad], axis=0).T
    for h in range(hh):
        s_new = cols[:, h:h + 1] * s0_ref[0, h] + cols[:, hh + h:hh + h + 1] * i8[h:h + 1, :]
        s_ref[0, h] = s_new
        qb = jnp.broadcast_to(q8[h:h + 1, :], (8, HGRN_DK)).astype(BF16)
        o = _dot(qb, s_new.astype(BF16))[0:1, :]
        y_ref[0, h:h + 1, :] = _rms_gate(o, gain_ref[...], g8[h:h + 1, :])


def hgrn_sample(proj, s0, lb_logits, gain, layer):
    bsz = proj.shape[0]
    hh = HGRN_HEADS
    nl = lb_logits.shape[0]
    y, s_new = pl.pallas_call(
        functools.partial(_hgrn_sample_kernel, layer=layer),
        grid=(bsz,),
        in_specs=[pl.BlockSpec((1, 4 * hh, HGRN_DK), lambda b: (b, 0, 0)),
                  pl.BlockSpec((1, hh, HGRN_DK, HGRN_DV), lambda b: (b, 0, 0, 0)),
                  pl.BlockSpec((nl, hh, HGRN_DK), lambda b: (0, 0, 0)),
                  pl.BlockSpec((1, HGRN_DV), lambda b: (0, 0))],
        out_specs=(pl.BlockSpec((1, hh, HGRN_DV), lambda b: (b, 0, 0)),
                   pl.BlockSpec((1, hh, HGRN_DK, HGRN_DV), lambda b: (b, 0, 0, 0))),
        out_shape=(jax.ShapeDtypeStruct((bsz, hh, HGRN_DV), F32),
                   jax.ShapeDtypeStruct(s0.shape, F32)),
        compiler_params=_params("parallel"),
        name="hgrn_sample",
    )(proj.reshape(bsz, 4 * hh, HGRN_DK), s0, lb_logits.reshape(nl, hh, HGRN_DK), gain.reshape(1, HGRN_DV))
    return y.reshape(bsz, hh * HGRN_DV), s_new


def _nsa_sample_attn_kernel(pt_ref, q_ref, kvn_ref, gp_ref, bg_ref, kc_ref, vc_ref, win_ref,
                            cmap_ref, g16_ref, e_ref, *rest, past, n_cmp, n_slc):
    del pt_ref
    page_refs, o_ref = rest[:-1], rest[-1]
    kw = KV_WIDTH
    q16 = q_ref[0] * ATTN_SCALE
    rowg = lax.broadcasted_iota(jnp.int32, (NSA_HEADS, 1), 0) // NSA_GROUP
    colg = lax.broadcasted_iota(jnp.int32, (1, kw), 1) // HEAD_DIM
    own = rowg == colg
    qbd = jnp.where(own, jnp.concatenate([q16] * NSA_KV_HEADS, axis=1), 0.0).astype(BF16)

    def pick(o_full):
        om = jnp.where(own, o_full, 0.0)
        out = om[:, 0:HEAD_DIM]
        for g in range(1, NSA_KV_HEADS):
            out = out + om[:, g * HEAD_DIM:(g + 1) * HEAD_DIM]
        return out

    def new_row(lo):
        return kvn_ref[0][:, lo:lo + kw]

    def attend(k_all, v_all, mask, k_new, v_new, new_ok):
        s = jnp.where(mask, _dot_nt(qbd, k_all), NEG_INF)
        m = jnp.max(s, axis=1, keepdims=True)
        if k_new is not None:
            k8 = jnp.broadcast_to(k_new.astype(BF16), (8, kw))
            s_new = jnp.where(new_ok, _dot_nt(qbd, k8)[:, 0:1], NEG_INF)
            m = jnp.maximum(m, s_new)
        e = jnp.where(mask, jnp.exp(s - m), 0.0)
        l = jnp.sum(e, axis=1, keepdims=True)
        o_full = _dot(e.astype(BF16), v_all)
        if k_new is not None:
            e_new = jnp.where(new_ok, jnp.exp(s_new - m), 0.0)
            l = l + e_new
            o_full = o_full + e_new * v_new
        l = jnp.where(l > 0.0, l, 1.0)
        return pick(o_full) / l, e / l

    q_pos = past
    ncp = kc_ref.shape[1]
    c = lax.broadcasted_iota(jnp.int32, (1, ncp), 1)
    cmask = (c * CMP_STRIDE + (CMP_LEN - 1) <= q_pos) & (c < n_cmp)
    o_cmp, p_cmp = attend(kc_ref[0], vc_ref[0], cmask, None, None, None)

    psum = sum(_dot(g16_ref[...], piece) for piece in _split3(p_cmp))
    imp = sum(_dot(piece, cmap_ref[...]) for piece in _split3(psum))
    nsp = cmap_ref.shape[1]
    blk = lax.broadcasted_iota(jnp.int32, (1, nsp), 1)
    cur = q_pos // SLC_BLOCK
    forced = (blk == 0) | (blk == cur) | (blk == cur - 1)
    allowed = (blk <= cur) & (blk < n_slc)
    score = jnp.where(allowed, imp + jnp.where(forced, FORCE_BONUS, 0.0), -jnp.inf)
    rank = jnp.zeros((NSA_HEADS, nsp), jnp.int32)
    for j in range(n_slc):
        other = score[:, j:j + 1]
        before = (other > score) | ((other == score) & (j < blk))
        rank = rank + before.astype(jnp.int32)
    sel = (rank < min(TOP_N, n_slc)) & allowed
    sel_keys = _dot(sel.astype(BF16), e_ref[...]) > 0.5

    pages = [r[0] for r in page_refs]
    k_all = jnp.concatenate([pg[:, 0:kw] for pg in pages], axis=0).astype(BF16)
    v_all = jnp.concatenate([pg[:, kw:2 * kw] for pg in pages], axis=0).astype(BF16)
    new_ok = sel[:, cur:cur + 1]
    o_slc, _ = attend(k_all, v_all, sel_keys, new_row(2 * kw), new_row(3 * kw), new_ok)

    wb = win_ref.shape[1]
    wpos = past - wb + lax.broadcasted_iota(jnp.int32, (1, wb), 1)
    wmask = (wpos <= q_pos) & (wpos > q_pos - WINDOW) & (wpos >= 0)
    win = win_ref[0]
    o_win, _ = attend(win[:, 0:kw].astype(BF16), win[:, kw:2 * kw].astype(BF16), wmask,
                      new_row(4 * kw), new_row(5 * kw), True)

    gates = jax.nn.sigmoid(gp_ref[0] + bg_ref[...])
    o_ref[0] = gates[:, 0:1] * o_cmp + gates[:, 1:2] * o_slc + gates[:, 2:3] * o_win


def nsa_sample_layer(x, cache_cmp, cache_slc, cache_win, page_table, w_in, b_gate, w_cmp, pe_cmp, w_out, g, b):
    bsz, d = x.shape
    n_pages = page_table.shape[1]
    past = n_pages * PAGE_SIZE
    w2 = 2 * KV_WIDTH
    n_in = w_in.shape[1]
    n_pad = -(-n_in // LANES) * LANES
    w_pad = jnp.concatenate([w_in, jnp.zeros((d, n_pad - n_in), w_in.dtype)], axis=1).astype(BF16)
    proj = linear(x, w_pad, tm=bsz, tn=n_pad)
    q = proj[:, :Q_WIDTH].reshape(bsz, NSA_HEADS, HEAD_DIM)
    kv_new = proj[:, Q_WIDTH:Q_WIDTH + 3 * w2]
    gate_pre = proj[:, Q_WIDTH + 3 * w2:n_in].reshape(bsz, 3, NSA_HEADS).transpose(0, 2, 1)
    bg = b_gate.reshape(3, NSA_HEADS).T

    kc, vc = nsa_compress_sample(cache_cmp, page_table, w_cmp, pe_cmp)
    ncp = kc.shape[1]
    n_cmp = (past - CMP_LEN) // CMP_STRIDE + 1
    assert (past + 1 - CMP_LEN) // CMP_STRIDE + 1 == n_cmp
    n_slc = -(-(past + 1) // SLC_BLOCK)
    nsp = -(-n_slc // LANES) * LANES
    cmap = _cmap_t(n_cmp, n_slc, nsp, ncp).T
    g16 = (jnp.arange(NSA_HEADS)[:, None] // NSA_GROUP == jnp.arange(NSA_HEADS)[None, :] // NSA_GROUP).astype(BF16)
    expand = (jnp.arange(nsp)[:, None] == jnp.arange(past)[None, :] // SLC_BLOCK).astype(BF16)

    pages = cache_slc.reshape(cache_slc.shape[0], PAGE_SIZE, w2)
    win = cache_win.reshape(bsz, cache_win.shape[1], w2)
    wb = win.shape[1]
    const = lambda a: pl.BlockSpec(a.shape, lambda b, pt: (0,) * a.ndim)
    per_b = lambda *shape: pl.BlockSpec((1,) + shape, lambda b, pt: (b,) + (0,) * len(shape))
    page_specs = [pl.BlockSpec((1, PAGE_SIZE, w2), lambda b, pt, k=k: (pt[b, k], 0, 0)) for k in range(n_pages)]
    o = pl.pallas_call(
        functools.partial(_nsa_sample_attn_kernel, past=past, n_cmp=n_cmp, n_slc=n_slc),
        grid_spec=pltpu.PrefetchScalarGridSpec(
            num_scalar_prefetch=1, grid=(bsz,),
            in_specs=[per_b(NSA_HEADS, HEAD_DIM), per_b(1, 3 * w2), per_b(NSA_HEADS, 3), const(bg),
                      per_b(ncp, KV_WIDTH), per_b(ncp, KV_WIDTH), per_b(wb, w2),
                      const(cmap), const(g16), const(expand)] + page_specs,
            out_specs=per_b(NSA_HEADS, HEAD_DIM)),
        out_shape=jax.ShapeDtypeStruct((bsz, NSA_HEADS, HEAD_DIM), F32),
        compiler_params=_params("parallel"),
        name="nsa_sample_attention",
    )(page_table, q, kv_new.reshape(bsz, 1, 3 * w2), gate_pre, bg, kc, vc, win, cmap, g16, expand,
      *([pages] * n_pages))
    y = proj_deepnorm(o.reshape(bsz, Q_WIDTH), w_out.astype(BF16), x, g, b, tm=bsz)
    return y, kv_new


def nsa_prompt_layer(x, w_in, b_gate, w_cmp, pe_cmp, w_out, g, b):
    kvc, kvs, kvw, ks, kw, qt, vst, vwt, gt = nsa_prompt_proj(x, w_in, b_gate, tq=512)
    kc, vct = nsa_compress_prompt(kvc, w_cmp, pe_cmp)
    o_t = nsa_prompt_attention(qt, gt, kc, vct, ks, vst, kw, vwt)
    return proj_t_deepnorm(o_t, w_out.astype(BF16), x, g, b), (kvc, kvs, kvw)


TM_PROJ = 512
TM_FFN = 1024
FFN_HC = 256
TM_GMLP = 256
TM_LINEAR = 512
TN_LINEAR = 1024


def kernel(x_prompt, x_sample, cache_cmp_kv, cache_slc_kv, cache_win_kv, state_hgrn, page_table, ln_gain, ln_bias, ffn_w_in, ffn_w_out, nsa_w_in, nsa_b_gate, nsa_w_cmp, nsa_pe_cmp, nsa_w_out, gmlp_w_in, gmlp_b_in, gmlp_ln_v, gmlp_w_sp, gmlp_b_sp, gmlp_w_out, hgrn_w_in, hgrn_lb_logits, hgrn_norm_gain, hgrn_w_out):
    bsz, t, d = x_prompt.shape
    sb = x_sample.shape[0]
    assert x_sample.shape[1] == 1
    xp = x_prompt
    xs = x_sample.reshape(sb, d)
    kv_shape = (2, NSA_KV_HEADS, HEAD_DIM)
    wb_prompt = min(WINDOW, t)
    cmp_p, cmp_s, slc_p, slc_s, win_p, win_s, gv_s, hs_p, hs_s = [], [], [], [], [], [], [], [], []
    for layer in range(DEPTH):
        kind, j = layer % N_MIXERS, layer // N_MIXERS
        g0, b0 = ln_gain[layer, 0], ln_bias[layer, 0]
        if kind == 0:
            xp, (kvc, kvs, kvw) = nsa_prompt_layer(xp, nsa_w_in[j], nsa_b_gate[j], nsa_w_cmp[j], nsa_pe_cmp[j],
                                                    nsa_w_out[j], g0, b0)
            xs, kv_new = nsa_sample_layer(xs, cache_cmp_kv[j], cache_slc_kv[j], cache_win_kv[j], page_table,
                                          nsa_w_in[j], nsa_b_gate[j], nsa_w_cmp[j], nsa_pe_cmp[j], nsa_w_out[j],
                                          g0, b0)
            page_shape = (bsz * t // PAGE_SIZE, PAGE_SIZE) + kv_shape
            cmp_p.append(kvc.reshape(page_shape))
            slc_p.append(kvs.reshape(page_shape))
            win_p.append(kvw[:, t - wb_prompt:].reshape((bsz, wb_prompt) + kv_shape))
            w2 = 2 * KV_WIDTH
            cmp_s.append(kv_new[:, 0:w2].reshape((sb, 1) + kv_shape))
            slc_s.append(kv_new[:, w2:2 * w2].reshape((sb, 1) + kv_shape))
            win_s.append(kv_new[:, 2 * w2:3 * w2].reshape((sb, 1) + kv_shape))
        elif kind == 1:
            w_in_bf, w_out_bf = gmlp_w_in[j].astype(BF16), gmlp_w_out[j].astype(BF16)
            xp = gmlp_layer(xp.reshape(bsz * t, d), w_in_bf, gmlp_b_in[j], gmlp_ln_v[j], gmlp_w_sp[j], gmlp_b_sp[j],
                            w_out_bf, g0, b0, chunked=True, tm=TM_GMLP).reshape(bsz, t, d)
            xs, v_new = gmlp_layer(xs, w_in_bf, gmlp_b_in[j], gmlp_ln_v[j], gmlp_w_sp[j], gmlp_b_sp[j],
                                   w_out_bf, g0, b0, chunked=False, tm=sb)
            gv_s.append(v_new.reshape(sb, 1, GMLP_WIDTH))
        else:
            w_in_bf, w_out_bf = hgrn_w_in[j].astype(BF16), hgrn_w_out[j].astype(BF16)
            proj = linear(xp.reshape(bsz * t, d), w_in_bf, TM_LINEAR, TN_LINEAR).reshape(bsz, t, 4 * HGRN_WIDTH)
            y, s_p = hgrn_prompt(proj, hgrn_lb_logits, hgrn_norm_gain[j], layer)
            xp = proj_deepnorm(y.reshape(bsz * t, HGRN_WIDTH), w_out_bf, xp.reshape(bsz * t, d), g0, b0,
                               TM_PROJ).reshape(bsz, t, d)
            proj_s = linear(xs, w_in_bf, sb, TN_LINEAR)
            y_s, s_s = hgrn_sample(proj_s, state_hgrn[j], hgrn_lb_logits, hgrn_norm_gain[j], layer)
            xs = proj_deepnorm(y_s, w_out_bf, xs, g0, b0, sb)
            hs_p.append(s_p)
            hs_s.append(s_s)
        w_in_bf, w_out_bf = ffn_w_in[layer].astype(BF16), ffn_w_out[layer].astype(BF16)
        g1, b1 = ln_gain[layer, 1], ln_bias[layer, 1]
        xp = ffn_deepnorm(xp.reshape(bsz * t, d), w_in_bf, w_out_bf, g1, b1, TM_FFN, FFN_HC).reshape(bsz, t, d)
        xs = ffn_deepnorm(xs, w_in_bf, w_out_bf, g1, b1, sb, FFN_HC)
    return (xp, xs.reshape(sb, 1, d), jnp.stack(cmp_p), jnp.stack(cmp_s), jnp.stack(slc_p), jnp.stack(slc_s),
            jnp.stack(win_p), jnp.stack(win_s), jnp.stack(gv_s), jnp.stack(hs_p), jnp.stack(hs_s))
```

```python
import functools

import jax
import jax.numpy as jnp
import numpy as np
from jax import lax
from jax.experimental import pallas as pl
from jax.experimental.pallas import tpu as pltpu

F32 = jnp.float32
BF16 = jnp.bfloat16

D_MODEL = 1024
DEPTH = 4
PAGE_SIZE = 128
N_MIXERS = 3

NSA_HEADS = 16
NSA_KV_HEADS = 4
NSA_GROUP = NSA_HEADS // NSA_KV_HEADS
HEAD_DIM = D_MODEL // NSA_HEADS
Q_WIDTH = NSA_HEADS * HEAD_DIM
KV_WIDTH = NSA_KV_HEADS * HEAD_DIM
CMP_LEN = 32
CMP_STRIDE = 16
SLC_BLOCK = 64
TOP_N = 16
WINDOW = 512
FORCE_BONUS = 1e3
ATTN_SCALE = HEAD_DIM ** -0.5
NEG_INF = -1e30

GMLP_WIDTH = D_MODEL
GMLP_GROUPS = 8
GMLP_GROUP_DIM = GMLP_WIDTH // GMLP_GROUPS
GMLP_CHUNK = 128

HGRN_HEADS = 8
HGRN_DK = 128
HGRN_DV = 128
HGRN_WIDTH = HGRN_HEADS * HGRN_DK

FFN_HIDDEN = 2816
DEEPNORM_ALPHA = (2 * DEPTH) ** 0.25
LN_EPS = 1e-5

LANES = 128
BF16_ROWS = 16
VMEM_LIMIT = 48 * 1024 * 1024


def _params(*sem):
    return pltpu.CompilerParams(dimension_semantics=sem, vmem_limit_bytes=VMEM_LIMIT)


def _dot(a, b):
    return jnp.dot(a, b, preferred_element_type=F32)


def _dot_nt(a, b):
    return lax.dot_general(a, b, (((1,), (1,)), ((), ())), preferred_element_type=F32)


def _layer_norm_rows(y, g, b):
    mu = jnp.mean(y, axis=-1, keepdims=True)
    yc = y - mu
    var = jnp.mean(yc * yc, axis=-1, keepdims=True)
    return yc * lax.rsqrt(var + LN_EPS) * g + b


def _deepnorm_rows(x, h, g, b):
    return _layer_norm_rows(DEEPNORM_ALPHA * x + h, g, b)


def _split3(x):
    hi = x.astype(BF16)
    r1 = x - hi.astype(F32)
    mid = r1.astype(BF16)
    lo = (r1 - mid.astype(F32)).astype(BF16)
    return hi, mid, lo


def _proj_deepnorm_kernel(a_ref, w_ref, x_ref, g_ref, b_ref, o_ref):
    h = _dot(a_ref[...].astype(BF16), w_ref[...])
    o_ref[...] = _deepnorm_rows(x_ref[...], h, g_ref[...], b_ref[...])


def proj_deepnorm(a, w_bf, x, g, b, tm):
    m, k = a.shape
    d = x.shape[1]
    tm = min(tm, m)
    return pl.pallas_call(
        _proj_deepnorm_kernel,
        grid=(m // tm,),
        in_specs=[pl.BlockSpec((tm, k), lambda i: (i, 0)),
                  pl.BlockSpec((k, d), lambda i: (0, 0)),
                  pl.BlockSpec((tm, d), lambda i: (i, 0)),
                  pl.BlockSpec((1, d), lambda i: (0, 0)),
                  pl.BlockSpec((1, d), lambda i: (0, 0))],
        out_specs=pl.BlockSpec((tm, d), lambda i: (i, 0)),
        out_shape=jax.ShapeDtypeStruct((m, d), F32),
        compiler_params=_params("parallel"),
        name="proj_deepnorm",
    )(a, w_bf, x, g.reshape(1, d), b.reshape(1, d))


def _ffn_kernel(x_ref, wg_ref, wu_ref, wo_ref, g_ref, b_ref, o_ref, xb_ref, acc_ref):
    j = pl.program_id(1)

    @pl.when(j == 0)
    def _():
        xb_ref[...] = x_ref[...].astype(BF16)
        acc_ref[...] = jnp.zeros_like(acc_ref)

    xb = xb_ref[...]
    gate = _dot(xb, wg_ref[...])
    up = _dot(xb, wu_ref[...])
    mid = (gate * jax.nn.sigmoid(gate) * up).astype(BF16)
    acc_ref[...] += _dot(mid, wo_ref[...])

    @pl.when(j == pl.num_programs(1) - 1)
    def _():
        o_ref[...] = _deepnorm_rows(x_ref[...], acc_ref[...], g_ref[...], b_ref[...])


def ffn_deepnorm(x, w_in_bf, w_out_bf, layer, g, b, tm, hc):
    m, d = x.shape
    hidden = w_out_bf.shape[1]
    tm = min(tm, m)
    nh = hidden // hc
    return pl.pallas_call(
        _ffn_kernel,
        grid=(m // tm, nh),
        in_specs=[pl.BlockSpec((tm, d), lambda i, j: (i, 0)),
                  pl.BlockSpec((None, d, hc), lambda i, j: (layer, 0, j)),
                  pl.BlockSpec((None, d, hc), lambda i, j: (layer, 0, nh + j)),
                  pl.BlockSpec((None, hc, d), lambda i, j: (layer, j, 0)),
                  pl.BlockSpec((1, d), lambda i, j: (0, 0)),
                  pl.BlockSpec((1, d), lambda i, j: (0, 0))],
        out_specs=pl.BlockSpec((tm, d), lambda i, j: (i, 0)),
        out_shape=jax.ShapeDtypeStruct((m, d), F32),
        scratch_shapes=[pltpu.VMEM((tm, d), BF16), pltpu.VMEM((tm, d), F32)],
        compiler_params=_params("parallel", "arbitrary"),
        name="ffn_deepnorm",
    )(x, w_in_bf, w_in_bf, w_out_bf, g.reshape(1, d), b.reshape(1, d))


def _gelu_exact(z):
    return 0.5 * z * (1.0 + lax.erf(z * (0.5 ** 0.5)))


def _gmlp_kernel(x_ref, win_ref, bin_ref, lnv_ref, wsp_ref, bsp_ref, wout_ref, g_ref, b_ref,
                 o_ref, *rest, chunked):
    width = GMLP_WIDTH
    x = x_ref[...]
    z = _dot(x.astype(BF16), win_ref[...]) + bin_ref[...]
    gz = _gelu_exact(z)
    u = gz[:, :width]
    v = _layer_norm_rows(gz[:, width:], lnv_ref[0:1, :], lnv_ref[1:2, :])
    if chunked:
        mix_ref, = rest
        vb = v.astype(BF16)
        for c in range(x.shape[0] // GMLP_CHUNK):
            rows = slice(c * GMLP_CHUNK, (c + 1) * GMLP_CHUNK)
            for h in range(GMLP_GROUPS):
                cols = slice(h * GMLP_GROUP_DIM, (h + 1) * GMLP_GROUP_DIM)
                mix_ref[rows, cols] = _dot(wsp_ref[h], vb[rows, cols]) + bsp_ref[:, cols]
        mixed = mix_ref[...]
    else:
        v_ref, = rest
        v_ref[...] = v
        mixed = v * wsp_ref[...] + bsp_ref[...]
    h_out = _dot((u * mixed).astype(BF16), wout_ref[...])
    o_ref[...] = _deepnorm_rows(x, h_out, g_ref[...], b_ref[...])


def gmlp_layer(x, w_in_bf, b_in, ln_v, w_sp, b_sp, w_out_bf, g, b, *, chunked, tm):
    m, d = x.shape
    width = GMLP_WIDTH
    tm = min(tm, m)
    const = lambda *shape: pl.BlockSpec(shape, lambda i: (0,) * len(shape))
    if chunked:
        wsp = jnp.tril(w_sp).astype(BF16)
        bsp = jnp.repeat(b_sp.T, GMLP_GROUP_DIM, axis=1)
        wsp_spec, bsp_spec = const(GMLP_GROUPS, GMLP_CHUNK, GMLP_CHUNK), const(GMLP_CHUNK, width)
        out_shape = jax.ShapeDtypeStruct((m, d), F32)
        out_specs = pl.BlockSpec((tm, d), lambda i: (i, 0))
        scratch = [pltpu.VMEM((tm, width), F32)]
    else:
        wsp = jnp.repeat(w_sp[:, 0, 0], GMLP_GROUP_DIM).reshape(1, width)
        bsp = jnp.repeat(b_sp[:, 0], GMLP_GROUP_DIM).reshape(1, width)
        wsp_spec, bsp_spec = const(1, width), const(1, width)
        out_shape = (jax.ShapeDtypeStruct((m, d), F32), jax.ShapeDtypeStruct((m, width), F32))
        out_specs = (pl.BlockSpec((tm, d), lambda i: (i, 0)), pl.BlockSpec((tm, width), lambda i: (i, 0)))
        scratch = []
    return pl.pallas_call(
        functools.partial(_gmlp_kernel, chunked=chunked),
        grid=(m // tm,),
        in_specs=[pl.BlockSpec((tm, d), lambda i: (i, 0)),
                  const(d, 2 * width), const(1, 2 * width), const(2, width),
                  wsp_spec, bsp_spec, const(width, d), const(1, d), const(1, d)],
        out_specs=out_specs,
        out_shape=out_shape,
        scratch_shapes=scratch,
        compiler_params=_params("parallel"),
        name="gmlp_layer",
    )(x, w_in_bf, b_in.reshape(1, 2 * width), ln_v, wsp, bsp, w_out_bf, g.reshape(1, d), b.reshape(1, d))


def _linear_kernel(x_ref, w_ref, o_ref):
    o_ref[...] = _dot(x_ref[...].astype(BF16), w_ref[...])


def linear(x, w_bf, tm, tn):
    m, k = x.shape
    n = w_bf.shape[1]
    tm, tn = min(tm, m), min(tn, n)
    return pl.pallas_call(
        _linear_kernel,
        grid=(m // tm, n // tn),
        in_specs=[pl.BlockSpec((tm, k), lambda i, j: (i, 0)),
                  pl.BlockSpec((k, tn), lambda i, j: (0, j))],
        out_specs=pl.BlockSpec((tm, tn), lambda i, j: (i, j)),
        out_shape=jax.ShapeDtypeStruct((m, n), F32),
        compiler_params=_params("parallel", "parallel"),
        name="linear",
    )(x, w_bf)


def _token_minor(cache):
    nd = cache.ndim
    return jnp.transpose(cache, tuple(range(nd - 4)) + (nd - 3, nd - 2, nd - 1, nd - 4))


def _token_major(kvt):
    nd = kvt.ndim
    return jnp.transpose(kvt, tuple(range(nd - 4)) + (nd - 1, nd - 4, nd - 3, nd - 2))


CMP_SLABS = 2 * KV_WIDTH // LANES


def _cmp_weights(w_cmp):
    w = w_cmp.reshape(2, 2, CMP_STRIDE, HEAD_DIM, HEAD_DIM)
    eye = jnp.eye(NSA_KV_HEADS, dtype=w_cmp.dtype)
    wbd = jnp.einsum('gh,klpde->pkgdlhe', eye, w)
    return wbd.reshape(CMP_STRIDE, 2, KV_WIDTH, 2 * KV_WIDTH).astype(BF16)


def _cmp_operands(w_cmp, pe_cmp, k_natural):
    wbd = _cmp_weights(w_cmp)
    wk = wbd[:, 0] if k_natural else jnp.swapaxes(wbd[:, 0], 1, 2)
    wv = jnp.swapaxes(wbd[:, 1], 1, 2)
    pe = pe_cmp.reshape(2, CMP_LEN * HEAD_DIM)
    wflat = w_cmp.reshape(2, CMP_LEN * HEAD_DIM, HEAD_DIM).astype(BF16)
    wflat_t = jnp.swapaxes(wflat, 1, 2)
    const = lambda a: pl.BlockSpec(a.shape, lambda *_: (0,) * a.ndim)
    ops = (wk, wv, pe, wflat, wflat_t)
    return ops, [const(a) for a in ops]


def _compress(load_rows, nhb, n_cmp, wk_ref, wv_ref, pe_ref, wflat_ref, wflat_t_ref, k_natural):
    kw = KV_WIDTH
    acc_k = jnp.zeros((nhb, 2 * kw) if k_natural else (2 * kw, nhb), F32)
    acc_v = jnp.zeros((2 * kw, nhb), F32)
    for p in range(CMP_STRIDE):
        slabs = [load_rows(p, s).astype(BF16) for s in range(CMP_SLABS)]
        xk = jnp.concatenate(slabs[:CMP_SLABS // 2], axis=1)
        xv = jnp.concatenate(slabs[CMP_SLABS // 2:], axis=1)
        acc_k = acc_k + (_dot(xk, wk_ref[p]) if k_natural else _dot_nt(wk_ref[p], xk))
        acc_v = acc_v + _dot_nt(wv_ref[p], xv)

    def pe8(kv):
        return jnp.broadcast_to(pe_ref[kv:kv + 1, :], (8, CMP_LEN * HEAD_DIM)).astype(BF16)

    def finish_t(acc, kv):
        bias = _dot_nt(wflat_t_ref[kv], pe8(kv))[:, 0:1]
        bias = jnp.concatenate([bias] * NSA_KV_HEADS, axis=0)
        out = acc[:kw, :] + pltpu.roll(acc[kw:, :], nhb - 1, 1) + bias
        return jnp.where(lax.broadcasted_iota(jnp.int32, (1, nhb), 1) < n_cmp, out, 0.0)

    if k_natural:
        bias = _dot(pe8(0), wflat_ref[0])[0:1, :]
        bias = jnp.concatenate([bias] * NSA_KV_HEADS, axis=1)
        k_out = acc_k[:, :kw] + pltpu.roll(acc_k[:, kw:], nhb - 1, 0) + bias
        k_out = jnp.where(lax.broadcasted_iota(jnp.int32, (nhb, 1), 0) < n_cmp, k_out, 0.0)
    else:
        k_out = finish_t(acc_k, 0)
    return k_out, finish_t(acc_v, 1)


def _cmap_t(n_cmp, n_slc, rows, cols):
    c0 = np.arange(cols)[None, :] * CMP_STRIDE
    s0 = np.arange(rows)[:, None] * SLC_BLOCK
    ov = np.minimum(c0 + CMP_LEN, s0 + SLC_BLOCK) - np.maximum(c0, s0)
    ov = np.clip(ov, 0, None).astype(np.float32) / CMP_STRIDE
    keep = (np.arange(cols)[None, :] < n_cmp) & (np.arange(rows)[:, None] < n_slc)
    return np.where(keep, ov, 0.0).astype(np.float32)


def _rank_before(score, n_slc, axis, blk):
    rank = jnp.zeros(score.shape, jnp.int32)
    for j in range(n_slc):
        other = score[j:j + 1, :] if axis == 0 else score[:, j:j + 1]
        before = (other > score) | ((other == score) & (j < blk))
        rank = rank + before.astype(jnp.int32)
    return rank


NSA_QB = 128
NSA_KT = 2 * NSA_QB
NSA_GATE_ROWS = 64
NSA_T_ROWS = Q_WIDTH + 6 * KV_WIDTH + NSA_GATE_ROWS
NSA_NAT_COLS = 4 * KV_WIDTH


def _nsa_proj_kernel(x_ref, wn_ref, wt_ref, bg_ref, *rest, n_prev):
    xc_ref, ks_ref, kw_ref, qt_ref, vst_ref, vwt_ref, gt_ref, kvc_ref, kvs_ref, kvw_ref = rest[n_prev:]
    tq = x_ref.shape[1]
    kw_, hd = KV_WIDTH, HEAD_DIM
    xb = x_ref[0].astype(BF16)
    nat = _dot(xb, wn_ref[...])
    for s in range(CMP_SLABS):
        xc_ref[0, s] = nat[:, s * LANES:(s + 1) * LANES]
    tok = pl.program_id(1) * tq + lax.broadcasted_iota(jnp.int32, (tq, hd), 0)
    onehot = (tok // SLC_BLOCK == lax.broadcasted_iota(jnp.int32, (tq, hd), 1)).astype(BF16)
    for g in range(NSA_KV_HEADS):
        k_slc = nat[:, 2 * kw_ + g * hd:2 * kw_ + (g + 1) * hd].astype(BF16)
        ks_ref[0, g] = jnp.concatenate([k_slc, onehot], axis=1)
        kw_ref[0, g] = nat[:, 3 * kw_ + g * hd:3 * kw_ + (g + 1) * hd].astype(BF16)
    t = _dot_nt(wt_ref[...], xb)
    kv0 = Q_WIDTH
    for g in range(NSA_KV_HEADS):
        r_slc = kv0 + 3 * kw_ + g * hd
        r_win = kv0 + 5 * kw_ + g * hd
        vst_ref[0, g] = t[r_slc:r_slc + hd, :].astype(BF16)
        vwt_ref[0, g] = t[r_win:r_win + hd, :].astype(BF16)
    gates = jax.nn.sigmoid(t[kv0 + 6 * kw_:, :] + bg_ref[...])
    for nb in range(tq // NSA_QB):
        cols = slice(nb * NSA_QB, (nb + 1) * NSA_QB)
        gt_ref[0, nb] = gates[:, cols]
        for br, ref in enumerate((kvc_ref, kvs_ref, kvw_ref)):
            rows = t[kv0 + br * 2 * kw_:kv0 + (br + 1) * 2 * kw_, cols]
            ref[0, nb] = rows.reshape(2, NSA_KV_HEADS, hd, NSA_QB)
        for g in range(NSA_KV_HEADS):
            for r in range(NSA_GROUP):
                r0 = (g * NSA_GROUP + r) * hd
                qt_ref[0, nb, g, :, r * NSA_QB:(r + 1) * NSA_QB] = (
                    t[r0:r0 + hd, cols] * ATTN_SCALE).astype(BF16)


def nsa_prompt_proj(x, w_in, b_gate, tq, layer, n_layers, prev_pages):
    bsz, t, d = x.shape
    assert NSA_QB == PAGE_SIZE and t // SLC_BLOCK <= HEAD_DIM
    nb, nbk = t // NSA_QB, tq // NSA_QB
    g_, hd, w2 = NSA_KV_HEADS, HEAD_DIM, 2 * KV_WIDTH
    w_q = w_in[:, :Q_WIDTH]
    w_kv = w_in[:, Q_WIDTH:Q_WIDTH + 3 * w2]
    w_g = w_in[:, Q_WIDTH + 3 * w2:]
    pad = jnp.zeros((d, NSA_GATE_ROWS - 3 * NSA_HEADS), w_in.dtype)
    w_t = jnp.concatenate([w_q, w_kv, w_g, pad], axis=1).T.astype(BF16)
    w_nat = jnp.concatenate([w_kv[:, :w2], w_kv[:, w2:w2 + KV_WIDTH], w_kv[:, 2 * w2:2 * w2 + KV_WIDTH]],
                            axis=1).astype(BF16)
    bg = jnp.concatenate([b_gate, jnp.zeros((NSA_GATE_ROWS - 3 * NSA_HEADS,), b_gate.dtype)]).reshape(NSA_GATE_ROWS, 1)
    page = jax.ShapeDtypeStruct((n_layers, bsz, nb, 2, g_, hd, NSA_QB), F32)
    page_spec = pl.BlockSpec((None, 1, nbk, 2, g_, hd, NSA_QB), lambda b, i: (layer, b, i, 0, 0, 0, 0))
    out_shape = (
        jax.ShapeDtypeStruct((bsz, CMP_SLABS, t, LANES), F32),
        jax.ShapeDtypeStruct((bsz, g_, t, 2 * hd), BF16), jax.ShapeDtypeStruct((bsz, g_, t, hd), BF16),
        jax.ShapeDtypeStruct((bsz, nb, g_, hd, NSA_GROUP * NSA_QB), BF16),
        jax.ShapeDtypeStruct((bsz, g_, hd, t), BF16), jax.ShapeDtypeStruct((bsz, g_, hd, t), BF16),
        jax.ShapeDtypeStruct((bsz, nb, NSA_GATE_ROWS, NSA_QB), F32),
        page, page, page,
    )
    out_specs = (
        pl.BlockSpec((1, CMP_SLABS, tq, LANES), lambda b, i: (b, 0, i, 0)),
        pl.BlockSpec((1, g_, tq, 2 * hd), lambda b, i: (b, 0, i, 0)),
        pl.BlockSpec((1, g_, tq, hd), lambda b, i: (b, 0, i, 0)),
        pl.BlockSpec((1, nbk, g_, hd, NSA_GROUP * NSA_QB), lambda b, i: (b, i, 0, 0, 0)),
        pl.BlockSpec((1, g_, hd, tq), lambda b, i: (b, 0, 0, i)),
        pl.BlockSpec((1, g_, hd, tq), lambda b, i: (b, 0, 0, i)),
        pl.BlockSpec((1, nbk, NSA_GATE_ROWS, NSA_QB), lambda b, i: (b, i, 0, 0)),
        page_spec, page_spec, page_spec,
    )
    prev = tuple(prev_pages) if prev_pages is not None else ()
    n_out = len(out_shape)
    aliases = {4 + i: n_out - len(prev) + i for i in range(len(prev))}
    return pl.pallas_call(
        functools.partial(_nsa_proj_kernel, n_prev=len(prev)),
        grid=(bsz, t // tq),
        in_specs=[pl.BlockSpec((1, tq, d), lambda b, i: (b, i, 0)),
                  pl.BlockSpec((d, NSA_NAT_COLS), lambda b, i: (0, 0)),
                  pl.BlockSpec((NSA_T_ROWS, d), lambda b, i: (0, 0)),
                  pl.BlockSpec((NSA_GATE_ROWS, 1), lambda b, i: (0, 0))]
                 + [pl.BlockSpec(memory_space=pl.ANY)] * len(prev),
        out_specs=out_specs,
        out_shape=out_shape,
        input_output_aliases=aliases,
        compiler_params=_params("parallel", "parallel"),
        name="nsa_prompt_proj",
    )(x, w_nat, w_t, bg, *prev)


def _nsa_compress_prompt_kernel(x_ref, wk_ref, wv_ref, pe_ref, wflat_ref, wflat_t_ref, kc_ref, vct_ref, *, n_cmp):
    nhb = x_ref.shape[2] // CMP_STRIDE
    load = lambda p, s: x_ref[0, s, pl.ds(p, nhb, stride=CMP_STRIDE), :]
    k_all, v_t = _compress(load, nhb, n_cmp, wk_ref, wv_ref, pe_ref, wflat_ref, wflat_t_ref, True)
    for g in range(NSA_KV_HEADS):
        kc_ref[0, g] = k_all[:, g * HEAD_DIM:(g + 1) * HEAD_DIM].astype(BF16)
        vct_ref[0, g] = v_t[g * HEAD_DIM:(g + 1) * HEAD_DIM, :].astype(BF16)


def nsa_compress_prompt(kv_cmp, w_cmp, pe_cmp):
    bsz, _, t, _ = kv_cmp.shape
    nhb = t // CMP_STRIDE
    n_cmp = (t - CMP_LEN) // CMP_STRIDE + 1
    ops, specs = _cmp_operands(w_cmp, pe_cmp, True)
    return pl.pallas_call(
        functools.partial(_nsa_compress_prompt_kernel, n_cmp=n_cmp),
        grid=(bsz,),
        in_specs=[pl.BlockSpec((1, CMP_SLABS, t, LANES), lambda b: (b, 0, 0, 0))] + specs,
        out_specs=(pl.BlockSpec((1, NSA_KV_HEADS, nhb, HEAD_DIM), lambda b: (b, 0, 0, 0)),
                   pl.BlockSpec((1, NSA_KV_HEADS, HEAD_DIM, nhb), lambda b: (b, 0, 0, 0))),
        out_shape=(jax.ShapeDtypeStruct((bsz, NSA_KV_HEADS, nhb, HEAD_DIM), BF16),
                   jax.ShapeDtypeStruct((bsz, NSA_KV_HEADS, HEAD_DIM, nhb), BF16)),
        compiler_params=_params("parallel"),
        name="nsa_compress_prompt",
    )(kv_cmp, *ops)


def _attn_bias_consts():
    k = np.arange(NSA_QB)[:, None]
    q = np.arange(NSA_QB)[None, :]
    x4 = lambda a: np.tile(a.astype(np.float32), (1, NSA_GROUP))
    causal = x4(np.where(k <= q, 0.0, NEG_INF))
    lower = x4(np.where(k > q, 0.0, NEG_INF))
    zeros, neg = np.zeros_like(causal), np.full_like(causal, NEG_INF)
    win_bias = np.stack([zeros, causal, neg, lower])
    slc_bias = np.stack([np.concatenate([causal, neg]), np.concatenate([zeros, causal])])
    return win_bias, slc_bias


WB_ZERO, WB_CAUSAL, WB_NEG, WB_LOWER = 0, 1, 2, 3


def _softmax_cols(s, mask):
    sm = jnp.where(mask, s, NEG_INF)
    m = jnp.max(sm, axis=0, keepdims=True)
    e = jnp.where(mask, jnp.exp(sm - m), 0.0)
    l = jnp.sum(e, axis=0, keepdims=True)
    return e / jnp.where(l > 0.0, l, 1.0)


def _lanes_x4(a):
    return jnp.concatenate([a] * NSA_GROUP, axis=1)


def _softmax_pv(s_ref, p_ref, n_rows, tile, vt):
    tiles = [slice(j * tile, (j + 1) * tile) for j in range(n_rows // tile)]
    m = jnp.max(s_ref[tiles[0], :], axis=0, keepdims=True)
    for rows in tiles[1:]:
        m = jnp.maximum(m, jnp.max(s_ref[rows, :], axis=0, keepdims=True))
    l = jnp.zeros_like(m)
    for rows in tiles:
        e = jnp.exp(s_ref[rows, :] - m)
        l = l + jnp.sum(e, axis=0, keepdims=True)
        p_ref[rows, :] = e.astype(BF16)
    return _dot(vt, p_ref[0:n_rows, :]) / l


def _nsa_prompt_attn_kernel(qt_ref, gt_ref, kc_ref, vct_ref, ks_ref, vst_ref, kw_ref, vwt_ref, cmap_ref,
                            wbias_ref, sbias_ref, o_ref, s_ref, p_ref, oslc_ref, *, n_cmp, n_slc, nb):
    qb, kt = NSA_QB, NSA_KT
    width = NSA_GROUP * qb
    g = pl.program_id(1)
    n = pl.program_id(2)
    s0 = n * qb
    qt = qt_ref[0, 0, 0]
    qpos1 = s0 + lax.broadcasted_iota(jnp.int32, (1, qb), 1)
    qpos = _lanes_x4(qpos1)

    ncp = kc_ref.shape[2]
    crow = lax.broadcasted_iota(jnp.int32, (ncp, 1), 0)
    cmask = (crow * CMP_STRIDE + (CMP_LEN - 1) <= qpos) & (crow < n_cmp)
    p_cmp = _softmax_cols(_dot(kc_ref[0, 0], qt), cmask)
    o_cmp = _dot(vct_ref[0, 0], p_cmp.astype(BF16))

    psum = p_cmp[:, 0:qb]
    for r in range(1, NSA_GROUP):
        psum = psum + p_cmp[:, r * qb:(r + 1) * qb]
    imp = sum(_dot(cmap_ref[...], piece) for piece in _split3(psum))
    nsp = cmap_ref.shape[0]
    blk = lax.broadcasted_iota(jnp.int32, (nsp, 1), 0)
    cur = qpos1 // SLC_BLOCK
    forced = (blk == 0) | (blk == cur) | (blk == cur - 1)
    allowed = (blk <= cur) & (blk < n_slc)
    score = jnp.where(allowed, imp + jnp.where(forced, FORCE_BONUS, 0.0), -jnp.inf)
    sel = (_rank_before(score, n_slc, 0, blk) < min(TOP_N, n_slc)) & allowed
    sel_bias = _lanes_x4(jnp.where(sel, 0.0, NEG_INF)).astype(BF16)
    q_ext = jnp.concatenate([qt, sel_bias, jnp.zeros((HEAD_DIM - nsp, width), BF16)], axis=0)

    last = n // (kt // qb)
    for k_last in range(nb // (kt // qb)):
        @pl.when(last == k_last)
        def _():
            n_rows = (k_last + 1) * kt
            s_ref[0:n_rows, :] = _dot(ks_ref[0, 0, 0:n_rows, :], q_ext)
            diag = slice(k_last * kt, n_rows)
            s_ref[diag, :] = s_ref[diag, :] + sbias_ref[n % (kt // qb)]
            oslc_ref[...] = _softmax_pv(s_ref, p_ref, n_rows, kt, vst_ref[0, 0, :, 0:n_rows])

    o_slc = oslc_ref[...]

    n_win = WINDOW // qb
    n_wt = min(n_win + 1, nb)
    first = jnp.clip(n - (n_wt - 1), 0, nb - n_wt)
    rows = pl.ds(pl.multiple_of(first * qb, qb), n_wt * qb)
    s_ref[0:n_wt * qb, :] = _dot(kw_ref[0, 0, rows, :], qt)
    for i in range(n_wt):
        j = first + i
        which = jnp.where(j > n, WB_NEG, jnp.where(j == n, WB_CAUSAL, jnp.where(j == n - n_win, WB_LOWER, WB_ZERO)))
        blk_rows = slice(i * qb, (i + 1) * qb)
        s_ref[blk_rows, :] = s_ref[blk_rows, :] + wbias_ref[which]
    o_win = _softmax_pv(s_ref, p_ref, n_wt * qb, qb, vwt_ref[0, 0, :, rows])

    def gate(branch):
        rows = gt_ref[0, 0, pl.ds(branch * NSA_HEADS + g * NSA_GROUP, NSA_GROUP), :]
        return jnp.concatenate([rows[r:r + 1, :] for r in range(NSA_GROUP)], axis=1)

    o = gate(0) * o_cmp + gate(1) * o_slc + gate(2) * o_win
    for r in range(NSA_GROUP):
        o_ref[0, 0, r * HEAD_DIM:(r + 1) * HEAD_DIM, :] = o[:, r * qb:(r + 1) * qb]


def nsa_prompt_attention(qt, gt, kc, vct, ks, vst, kw, vwt):
    bsz, nb, g_, hd, width = qt.shape
    t = kw.shape[2]
    assert t % NSA_KT == 0
    ncp = kc.shape[2]
    n_cmp = (t - CMP_LEN) // CMP_STRIDE + 1
    n_slc = -(-t // SLC_BLOCK)
    nsp = -(-n_slc // BF16_ROWS) * BF16_ROWS
    cmap = jnp.asarray(_cmap_t(n_cmp, n_slc, nsp, ncp), BF16)
    win_bias, slc_bias = _attn_bias_consts()
    per_bg = lambda *shape: pl.BlockSpec((1, 1) + shape, lambda b, g, n: (b, g, 0, 0))
    const = lambda a: pl.BlockSpec(a.shape, lambda b, g, n: (0,) * a.ndim)
    return pl.pallas_call(
        functools.partial(_nsa_prompt_attn_kernel, n_cmp=n_cmp, n_slc=n_slc, nb=nb),
        grid=(bsz, g_, nb),
        in_specs=[pl.BlockSpec((1, 1, 1, hd, width), lambda b, g, n: (b, n, g, 0, 0)),
                  pl.BlockSpec((1, 1, NSA_GATE_ROWS, NSA_QB), lambda b, g, n: (b, n, 0, 0)),
                  per_bg(ncp, hd), per_bg(hd, ncp),
                  per_bg(t, 2 * hd), per_bg(hd, t), per_bg(t, hd), per_bg(hd, t),
                  const(cmap), const(win_bias), const(slc_bias)],
        out_specs=pl.BlockSpec((1, 1, NSA_GROUP * hd, NSA_QB), lambda b, g, n: (b, n, g, 0)),
        out_shape=jax.ShapeDtypeStruct((bsz, nb, Q_WIDTH, NSA_QB), F32),
        scratch_shapes=[pltpu.VMEM((t, width), F32), pltpu.VMEM((t, width), BF16), pltpu.VMEM((hd, width), F32)],
        compiler_params=_params("parallel", "parallel", "arbitrary"),
        name="nsa_prompt_attention",
    )(qt, gt, kc, vct, ks, vst, kw, vwt, cmap, win_bias, slc_bias)


def _proj_t_deepnorm_kernel(at_ref, w_ref, x_ref, g_ref, b_ref, o_ref):
    a = at_ref[0, 0].T.astype(BF16)
    o_ref[0] = _deepnorm_rows(x_ref[0], _dot(a, w_ref[...]), g_ref[...], b_ref[...])


def proj_t_deepnorm(a_t, w_bf, x, g, b):
    bsz, nb, k, qb = a_t.shape
    d = x.shape[2]
    return pl.pallas_call(
        _proj_t_deepnorm_kernel,
        grid=(bsz, nb),
        in_specs=[pl.BlockSpec((1, 1, k, qb), lambda b, n: (b, n, 0, 0)),
                  pl.BlockSpec((k, d), lambda b, n: (0, 0)),
                  pl.BlockSpec((1, qb, d), lambda b, n: (b, n, 0)),
                  pl.BlockSpec((1, d), lambda b, n: (0, 0)),
                  pl.BlockSpec((1, d), lambda b, n: (0, 0))],
        out_specs=pl.BlockSpec((1, qb, d), lambda b, n: (b, n, 0)),
        out_shape=jax.ShapeDtypeStruct(x.shape, F32),
        compiler_params=_params("parallel", "parallel"),
        name="proj_t_deepnorm",
    )(a_t, w_bf, x, g.reshape(1, d), b.reshape(1, d))


def nsa_prompt_layer(x, w_in, b_gate, w_cmp, pe_cmp, w_out, g, b, layer, n_layers, prev_pages):
    xc, ks, kw, qt, vst, vwt, gt, kvc, kvs, kvw = nsa_prompt_proj(x, w_in, b_gate, 512, layer, n_layers, prev_pages)
    kc, vct = nsa_compress_prompt(xc, w_cmp, pe_cmp)
    o_t = nsa_prompt_attention(qt, gt, kc, vct, ks, vst, kw, vwt)
    return proj_t_deepnorm(o_t, w_out.astype(BF16), x, g, b), (kvc, kvs, kvw)


def _nsa_compress_sample_kernel(*refs, n_pages, n_cmp):
    page_refs = refs[1:1 + n_pages]
    wk_ref, wv_ref, pe_ref, wflat_ref, wflat_t_ref, perm_ref, kct_ref, vct_ref, rows_ref = refs[1 + n_pages:]
    kw = KV_WIDTH
    hb_page = PAGE_SIZE // CMP_STRIDE
    for k in range(n_pages):
        for kv in range(2):
            tile = page_refs[k][0, 0, kv].reshape(kw, PAGE_SIZE).astype(BF16)
            by_p = _dot(tile, perm_ref[...]).T
            for p in range(CMP_STRIDE):
                rows_ref[kv, p, k * hb_page:(k + 1) * hb_page, :] = by_p[p * hb_page:(p + 1) * hb_page, :]
    nhb = n_pages * hb_page
    slabs_kv = kw // LANES
    load = lambda p, s: rows_ref[s // slabs_kv, p, :, (s % slabs_kv) * LANES:(s % slabs_kv + 1) * LANES]
    k_t, v_t = _compress(load, nhb, n_cmp, wk_ref, wv_ref, pe_ref, wflat_ref, wflat_t_ref, False)
    kct_ref[0] = k_t.astype(BF16)
    vct_ref[0] = v_t.astype(BF16)


def nsa_compress_sample(cache_cmp_t, layer, page_table, w_cmp, pe_cmp):
    bsz, n_pages = page_table.shape
    nhb = n_pages * PAGE_SIZE // CMP_STRIDE
    n_cmp = (n_pages * PAGE_SIZE - CMP_LEN) // CMP_STRIDE + 1
    ops, specs = _cmp_operands(w_cmp, pe_cmp, False)
    tok = np.arange(PAGE_SIZE)
    perm = np.zeros((PAGE_SIZE, PAGE_SIZE), np.float32)
    perm[tok, (tok % CMP_STRIDE) * (PAGE_SIZE // CMP_STRIDE) + tok // CMP_STRIDE] = 1.0
    ops = ops + (jnp.asarray(perm, BF16),)
    specs = specs + [pl.BlockSpec(perm.shape, lambda *_: (0, 0))]
    page_block = (1, 1) + cache_cmp_t.shape[2:]
    page_specs = [pl.BlockSpec(page_block, lambda b, pt, k=k: (layer, pt[b, k], 0, 0, 0, 0)) for k in range(n_pages)]
    out = jax.ShapeDtypeStruct((bsz, KV_WIDTH, nhb), BF16)
    out_spec = pl.BlockSpec((1, KV_WIDTH, nhb), lambda b, pt: (b, 0, 0))
    return pl.pallas_call(
        functools.partial(_nsa_compress_sample_kernel, n_pages=n_pages, n_cmp=n_cmp),
        grid_spec=pltpu.PrefetchScalarGridSpec(
            num_scalar_prefetch=1, grid=(bsz,),
            in_specs=page_specs + specs, out_specs=(out_spec, out_spec),
            scratch_shapes=[pltpu.VMEM((2, CMP_STRIDE, nhb, KV_WIDTH), F32)]),
        out_shape=(out, out),
        compiler_params=_params("parallel"),
        name="nsa_compress_sample",
    )(page_table, *([cache_cmp_t] * n_pages), *ops)


def _nsa_sample_attn_kernel(pt_ref, q_ref, kvn_ref, gp_ref, bg_ref, kct_ref, vct_ref, win_ref,
                            cmap_ref, g16_ref, e_ref, *rest, past, n_cmp, n_slc):
    del pt_ref
    page_refs, o_ref = rest[:-1], rest[-1]
    kw = KV_WIDTH
    q16 = q_ref[0] * ATTN_SCALE
    rowg = lax.broadcasted_iota(jnp.int32, (NSA_HEADS, 1), 0) // NSA_GROUP
    colg = lax.broadcasted_iota(jnp.int32, (1, kw), 1) // HEAD_DIM
    own = rowg == colg
    qbd = jnp.where(own, jnp.concatenate([q16] * NSA_KV_HEADS, axis=1), 0.0).astype(BF16)

    def pick(o_full):
        om = jnp.where(own, o_full, 0.0)
        out = om[:, 0:HEAD_DIM]
        for g in range(1, NSA_KV_HEADS):
            out = out + om[:, g * HEAD_DIM:(g + 1) * HEAD_DIM]
        return out

    def new_row(lo):
        return kvn_ref[0][:, lo:lo + kw]

    def attend(k_t, v_t, mask, k_new, v_new, new_ok):
        s = jnp.where(mask, _dot(qbd, k_t), NEG_INF)
        m = jnp.max(s, axis=1, keepdims=True)
        if k_new is not None:
            k8 = jnp.broadcast_to(k_new.astype(BF16), (8, kw))
            s_new = jnp.where(new_ok, _dot_nt(qbd, k8)[:, 0:1], NEG_INF)
            m = jnp.maximum(m, s_new)
        e = jnp.where(mask, jnp.exp(s - m), 0.0)
        l = jnp.sum(e, axis=1, keepdims=True)
        o_full = _dot_nt(e.astype(BF16), v_t)
        if k_new is not None:
            e_new = jnp.where(new_ok, jnp.exp(s_new - m), 0.0)
            l = l + e_new
            o_full = o_full + e_new * v_new
        l = jnp.where(l > 0.0, l, 1.0)
        return pick(o_full) / l, e / l

    q_pos = past
    ncp = kct_ref.shape[2]
    c = lax.broadcasted_iota(jnp.int32, (1, ncp), 1)
    cmask = (c * CMP_STRIDE + (CMP_LEN - 1) <= q_pos) & (c < n_cmp)
    o_cmp, p_cmp = attend(kct_ref[0], vct_ref[0], cmask, None, None, None)

    psum = sum(_dot(g16_ref[...], piece) for piece in _split3(p_cmp))
    imp = sum(_dot(piece, cmap_ref[...]) for piece in _split3(psum))
    nsp = cmap_ref.shape[1]
    blk = lax.broadcasted_iota(jnp.int32, (1, nsp), 1)
    cur = q_pos // SLC_BLOCK
    forced = (blk == 0) | (blk == cur) | (blk == cur - 1)
    allowed = (blk <= cur) & (blk < n_slc)
    score = jnp.where(allowed, imp + jnp.where(forced, FORCE_BONUS, 0.0), -jnp.inf)
    sel = (_rank_before(score, n_slc, 1, blk) < min(TOP_N, n_slc)) & allowed
    sel_keys = _dot(sel.astype(BF16), e_ref[...]) > 0.5

    k_t = jnp.concatenate([r[0, 0, 0].reshape(kw, PAGE_SIZE) for r in page_refs], axis=1).astype(BF16)
    v_t = jnp.concatenate([r[0, 0, 1].reshape(kw, PAGE_SIZE) for r in page_refs], axis=1).astype(BF16)
    new_ok = sel[:, cur:cur + 1]
    o_slc, _ = attend(k_t, v_t, sel_keys, new_row(2 * kw), new_row(3 * kw), new_ok)

    wb = win_ref.shape[-1]
    wpos = past - wb + lax.broadcasted_iota(jnp.int32, (1, wb), 1)
    wmask = (wpos <= q_pos) & (wpos > q_pos - WINDOW) & (wpos >= 0)
    o_win, _ = attend(win_ref[0, 0, 0].reshape(kw, wb).astype(BF16), win_ref[0, 0, 1].reshape(kw, wb).astype(BF16),
                      wmask, new_row(4 * kw), new_row(5 * kw), True)

    gates = jax.nn.sigmoid(gp_ref[0] + bg_ref[...])
    o_ref[0] = gates[:, 0:1] * o_cmp + gates[:, 1:2] * o_slc + gates[:, 2:3] * o_win


def nsa_sample_layer(x, cache_cmp_t, cache_slc_t, cache_win_t, layer, page_table, w_in, b_gate, w_cmp, pe_cmp,
                     w_out, g, b):
    bsz, d = x.shape
    n_pages = page_table.shape[1]
    past = n_pages * PAGE_SIZE
    w2 = 2 * KV_WIDTH
    n_in = w_in.shape[1]
    n_pad = -(-n_in // LANES) * LANES
    w_pad = jnp.concatenate([w_in, jnp.zeros((d, n_pad - n_in), w_in.dtype)], axis=1).astype(BF16)
    proj = linear(x, w_pad, tm=bsz, tn=n_pad)
    q = proj[:, :Q_WIDTH].reshape(bsz, NSA_HEADS, HEAD_DIM)
    kv_new = proj[:, Q_WIDTH:Q_WIDTH + 3 * w2]
    gate_pre = proj[:, Q_WIDTH + 3 * w2:n_in].reshape(bsz, 3, NSA_HEADS).transpose(0, 2, 1)
    bg = b_gate.reshape(3, NSA_HEADS).T

    kct, vct = nsa_compress_sample(cache_cmp_t, layer, page_table, w_cmp, pe_cmp)
    ncp = kct.shape[2]
    n_cmp = (past - CMP_LEN) // CMP_STRIDE + 1
    assert (past + 1 - CMP_LEN) // CMP_STRIDE + 1 == n_cmp
    n_slc = -(-(past + 1) // SLC_BLOCK)
    nsp = -(-n_slc // LANES) * LANES
    cmap = jnp.asarray(_cmap_t(n_cmp, n_slc, nsp, ncp).T, BF16)
    heads = np.arange(NSA_HEADS)
    g16 = jnp.asarray(heads[:, None] // NSA_GROUP == heads[None, :] // NSA_GROUP, BF16)
    expand = jnp.asarray(np.arange(nsp)[:, None] == np.arange(past)[None, :] // SLC_BLOCK, BF16)

    wb = cache_win_t.shape[-1]
    const = lambda a: pl.BlockSpec(a.shape, lambda b, pt: (0,) * a.ndim)
    per_b = lambda *shape: pl.BlockSpec((1,) + shape, lambda b, pt: (b,) + (0,) * len(shape))
    page_block = (1, 1) + cache_slc_t.shape[2:]
    page_specs = [pl.BlockSpec(page_block, lambda b, pt, k=k: (layer, pt[b, k], 0, 0, 0, 0)) for k in range(n_pages)]
    win_spec = pl.BlockSpec((1, 1) + cache_win_t.shape[2:], lambda b, pt: (layer, b, 0, 0, 0, 0))
    o = pl.pallas_call(
        functools.partial(_nsa_sample_attn_kernel, past=past, n_cmp=n_cmp, n_slc=n_slc),
        grid_spec=pltpu.PrefetchScalarGridSpec(
            num_scalar_prefetch=1, grid=(bsz,),
            in_specs=[per_b(NSA_HEADS, HEAD_DIM), per_b(1, 3 * w2), per_b(NSA_HEADS, 3), const(bg),
                      per_b(KV_WIDTH, ncp), per_b(KV_WIDTH, ncp), win_spec,
                      const(cmap), const(g16), const(expand)] + page_specs,
            out_specs=per_b(NSA_HEADS, HEAD_DIM)),
        out_shape=jax.ShapeDtypeStruct((bsz, NSA_HEADS, HEAD_DIM), F32),
        compiler_params=_params("parallel"),
        name="nsa_sample_attention",
    )(page_table, q, kv_new.reshape(bsz, 1, 3 * w2), gate_pre, bg, kct, vct, cache_win_t, cmap, g16, expand,
      *([cache_slc_t] * n_pages))
    y = proj_deepnorm(o.reshape(bsz, Q_WIDTH), w_out.astype(BF16), x, g, b, tm=bsz)
    return y, kv_new


HGRN_C = 128
HGRN_SUB = 8
HGRN_LEVELS = (8, 16, 32, 64)


def _hgrn_lower_bound(logits, layer):
    e = jnp.exp(logits - jnp.max(logits, axis=0, keepdims=True))
    sm = e / jnp.sum(e, axis=0, keepdims=True)
    lb = jnp.zeros_like(sm[0])
    for j in range(1, layer + 1):
        lb = lb + sm[j]
    return lb


def _hgrn_gates(f_raw, lb):
    log_sig = jnp.minimum(f_raw, 0.0) - jnp.log1p(jnp.exp(-jnp.abs(f_raw)))
    a = jnp.log(lb)
    b = jnp.log1p(-lb) + log_sig
    log_f = jnp.maximum(a, b) + jnp.log1p(jnp.exp(-jnp.abs(a - b)))
    return log_f, (1.0 - lb) * (1.0 - jax.nn.sigmoid(f_raw))


def _rms_gate(o, gain, g_raw):
    o = o * lax.rsqrt(jnp.mean(o * o, axis=-1, keepdims=True) + LN_EPS) * gain
    return o * (g_raw * jax.nn.sigmoid(g_raw))


def _hgrn_prompt_kernel(q_ref, f_ref, i_ref, g_ref, lb_ref, gain_ref, tri_ref, lmask_ref,
                        y_ref, s_ref, st_ref, cum_ref, *, layer):
    c = HGRN_C
    nblk = c // HGRN_SUB
    lb = _hgrn_lower_bound(lb_ref[:, 0], layer)
    st_ref[...] = jnp.zeros_like(st_ref)
    rsub = lax.broadcasted_iota(jnp.int32, (c, 1), 0) % HGRN_SUB

    def chunk(ci, _):
        rows = pl.ds(pl.multiple_of(ci * c, c), c)
        q = q_ref[0, rows, :]
        v = i_ref[0, rows, :]
        lf, k = _hgrn_gates(f_ref[0, rows, :], lb)
        cum = sum(_dot(tri_ref[...], piece) for piece in _split3(lf))
        cum_ref[...] = cum
        ends = cum_ref[pl.ds(HGRN_SUB - 1, nblk, stride=HGRN_SUB), :]
        end_b = [jnp.broadcast_to(ends[i:i + 1, :], (HGRN_SUB, HGRN_DK)) for i in range(nblk)]
        zero_b = jnp.zeros((HGRN_SUB, HGRN_DK), F32)
        vb = v.astype(BF16)

        a_off = jnp.zeros((c, c), F32)
        for li, lev in enumerate(HGRN_LEVELS):
            per = lev // HGRN_SUB
            before = [end_b[(i // per) * per - 1] if i >= per else zero_b for i in range(nblk)]
            after = [end_b[(i // per) * per + per - 1] for i in range(nblk)]
            qd = q * jnp.exp(cum - jnp.concatenate(before, axis=0))
            kd = k * jnp.exp(jnp.concatenate(after, axis=0) - cum)
            a_off = a_off + _dot_nt(qd.astype(BF16), kd.astype(BF16)) * lmask_ref[li]
        o = _dot(a_off.astype(BF16), vb)

        for j in range(HGRN_SUB):
            if j == 0:
                w = q * k
                vj = v
            else:
                ok = rsub >= j
                w = jnp.where(ok, q * pltpu.roll(k, j, 0) * jnp.exp(cum - pltpu.roll(cum, j, 0)), 0.0)
                vj = pltpu.roll(v, j, 0)
            o = o + jnp.sum(w, axis=1, keepdims=True) * vj

        last = end_b[nblk - 1][0:1, :]
        o = o + _dot_nt((q * jnp.exp(cum)).astype(BF16), st_ref[...].astype(BF16))
        kd = (k * jnp.exp(last - cum)).astype(BF16)
        st_ref[...] = st_ref[...] * jnp.exp(last) + _dot(v.T.astype(BF16), kd)
        y_ref[0, rows, :] = _rms_gate(o, gain_ref[...], g_ref[0, rows, :])
        return 0

    lax.fori_loop(0, q_ref.shape[1] // c, chunk, 0)
    s_ref[0, 0] = st_ref[...].T


def hgrn_prompt(proj, lb_logits, gain, layer):
    bsz, t, _ = proj.shape
    c = HGRN_C
    hh = HGRN_HEADS
    tri = jnp.asarray(np.tril(np.ones((c, c), np.float32)), BF16)
    ti = np.arange(c)[:, None]
    si = np.arange(c)[None, :]
    lmask = jnp.asarray(np.stack([((ti // (2 * lev) == si // (2 * lev)) & ((ti // lev) % 2 == 1)
                                   & ((si // lev) % 2 == 0)) for lev in HGRN_LEVELS]), F32)
    col = lambda part: pl.BlockSpec((1, t, HGRN_DK), lambda b, h: (b, 0, part * hh + h))
    return pl.pallas_call(
        functools.partial(_hgrn_prompt_kernel, layer=layer),
        grid=(bsz, hh),
        in_specs=[col(0), col(1), col(2), col(3),
                  pl.BlockSpec((lb_logits.shape[0], 1, 1, HGRN_DK), lambda b, h: (0, h, 0, 0)),
                  pl.BlockSpec((1, HGRN_DV), lambda b, h: (0, 0)),
                  pl.BlockSpec((c, c), lambda b, h: (0, 0)),
                  pl.BlockSpec((len(HGRN_LEVELS), c, c), lambda b, h: (0, 0, 0))],
        out_specs=(pl.BlockSpec((1, t, HGRN_DV), lambda b, h: (b, 0, h)),
                   pl.BlockSpec((1, 1, HGRN_DK, HGRN_DV), lambda b, h: (b, h, 0, 0))),
        out_shape=(jax.ShapeDtypeStruct((bsz, t, hh * HGRN_DV), F32),
                   jax.ShapeDtypeStruct((bsz, hh, HGRN_DK, HGRN_DV), F32)),
        scratch_shapes=[pltpu.VMEM((HGRN_DV, HGRN_DK), F32), pltpu.VMEM((c, HGRN_DK), F32)],
        compiler_params=_params("parallel", "parallel"),
        name="hgrn_prompt",
    )(proj, proj, proj, proj, lb_logits.reshape(lb_logits.shape[0], hh, 1, HGRN_DK),
      gain.reshape(1, HGRN_DV), tri, lmask)


def _hgrn_sample_kernel(p_ref, s0_ref, lb_ref, gain_ref, y_ref, s_ref, *, layer):
    hh = HGRN_HEADS
    p = p_ref[0]
    q8, f8, i8, g8 = (p[j * hh:(j + 1) * hh, :] for j in range(4))
    lf8, k8 = _hgrn_gates(f8, _hgrn_lower_bound(lb_ref[...], layer))
    pad = jnp.zeros((LANES - 2 * hh, HGRN_DK), F32)
    cols = jnp.concatenate([jnp.exp(lf8), k8, pad], axis=0).T
    for h in range(hh):
        s_new = cols[:, h:h + 1] * s0_ref[0, h] + cols[:, hh + h:hh + h + 1] * i8[h:h + 1, :]
        s_ref[0, h] = s_new
        qb = jnp.broadcast_to(q8[h:h + 1, :], (8, HGRN_DK)).astype(BF16)
        o = _dot(qb, s_new.astype(BF16))[0:1, :]
        y_ref[0, h:h + 1, :] = _rms_gate(o, gain_ref[...], g8[h:h + 1, :])


def hgrn_sample(proj, s0_all, j, lb_logits, gain, layer):
    bsz = proj.shape[0]
    hh = HGRN_HEADS
    nl = lb_logits.shape[0]
    y, s_new = pl.pallas_call(
        functools.partial(_hgrn_sample_kernel, layer=layer),
        grid=(bsz,),
        in_specs=[pl.BlockSpec((1, 4 * hh, HGRN_DK), lambda b: (b, 0, 0)),
                  pl.BlockSpec((None, 1, hh, HGRN_DK, HGRN_DV), lambda b: (j, b, 0, 0, 0)),
                  pl.BlockSpec((nl, hh, HGRN_DK), lambda b: (0, 0, 0)),
                  pl.BlockSpec((1, HGRN_DV), lambda b: (0, 0))],
        out_specs=(pl.BlockSpec((1, hh, HGRN_DV), lambda b: (b, 0, 0)),
                   pl.BlockSpec((1, hh, HGRN_DK, HGRN_DV), lambda b: (b, 0, 0, 0))),
        out_shape=(jax.ShapeDtypeStruct((bsz, hh, HGRN_DV), F32),
                   jax.ShapeDtypeStruct(s0_all.shape[1:], F32)),
        compiler_params=_params("parallel"),
        name="hgrn_sample",
    )(proj.reshape(bsz, 4 * hh, HGRN_DK), s0_all, lb_logits.reshape(nl, hh, HGRN_DK), gain.reshape(1, HGRN_DV))
    return y.reshape(bsz, hh * HGRN_DV), s_new


TM_PROJ = 512
TM_FFN = 1024
FFN_HC = 256
TM_GMLP = 256
TM_LINEAR = 512
TN_LINEAR = 1024


def kernel(x_prompt, x_sample, cache_cmp_kv, cache_slc_kv, cache_win_kv, state_hgrn, page_table, ln_gain, ln_bias, ffn_w_in, ffn_w_out, nsa_w_in, nsa_b_gate, nsa_w_cmp, nsa_pe_cmp, nsa_w_out, gmlp_w_in, gmlp_b_in, gmlp_ln_v, gmlp_w_sp, gmlp_b_sp, gmlp_w_out, hgrn_w_in, hgrn_lb_logits, hgrn_norm_gain, hgrn_w_out):
    bsz, t, d = x_prompt.shape
    sb = x_sample.shape[0]
    assert x_sample.shape[1] == 1
    xp = x_prompt
    xs = x_sample.reshape(sb, d)
    kv_shape = (2, NSA_KV_HEADS, HEAD_DIM)
    wb_prompt = min(WINDOW, t)
    cmp_t, slc_t, win_t = _token_minor(cache_cmp_kv), _token_minor(cache_slc_kv), _token_minor(cache_win_kv)
    cmp_s, slc_s, win_s, gv_s, hs_p, hs_s = [], [], [], [], [], []
    n_nsa = (DEPTH + N_MIXERS - 1) // N_MIXERS
    kv_pages = None
    ffn_in_bf, ffn_out_bf = ffn_w_in.astype(BF16), ffn_w_out.astype(BF16)
    for layer in range(DEPTH):
        kind, j = layer % N_MIXERS, layer // N_MIXERS
        g0, b0 = ln_gain[layer, 0], ln_bias[layer, 0]
        if kind == 0:
            xp, kv_pages = nsa_prompt_layer(xp, nsa_w_in[j], nsa_b_gate[j], nsa_w_cmp[j], nsa_pe_cmp[j],
                                            nsa_w_out[j], g0, b0, j, n_nsa, kv_pages)
            xs, kv_new = nsa_sample_layer(xs, cmp_t, slc_t, win_t, j, page_table,
                                          nsa_w_in[j], nsa_b_gate[j], nsa_w_cmp[j], nsa_pe_cmp[j], nsa_w_out[j],
                                          g0, b0)
            w2 = 2 * KV_WIDTH
            cmp_s.append(kv_new[:, 0:w2].reshape((sb, 1) + kv_shape))
            slc_s.append(kv_new[:, w2:2 * w2].reshape((sb, 1) + kv_shape))
            win_s.append(kv_new[:, 2 * w2:3 * w2].reshape((sb, 1) + kv_shape))
        elif kind == 1:
            w_in_bf, w_out_bf = gmlp_w_in[j].astype(BF16), gmlp_w_out[j].astype(BF16)
            xp = gmlp_layer(xp.reshape(bsz * t, d), w_in_bf, gmlp_b_in[j], gmlp_ln_v[j], gmlp_w_sp[j], gmlp_b_sp[j],
                            w_out_bf, g0, b0, chunked=True, tm=TM_GMLP).reshape(bsz, t, d)
            xs, v_new = gmlp_layer(xs, w_in_bf, gmlp_b_in[j], gmlp_ln_v[j], gmlp_w_sp[j], gmlp_b_sp[j],
                                   w_out_bf, g0, b0, chunked=False, tm=sb)
            gv_s.append(v_new.reshape(sb, 1, GMLP_WIDTH))
        else:
            w_in_bf, w_out_bf = hgrn_w_in[j].astype(BF16), hgrn_w_out[j].astype(BF16)
            proj = linear(xp.reshape(bsz * t, d), w_in_bf, TM_LINEAR, TN_LINEAR).reshape(bsz, t, 4 * HGRN_WIDTH)
            y, s_p = hgrn_prompt(proj, hgrn_lb_logits, hgrn_norm_gain[j], layer)
            xp = proj_deepnorm(y.reshape(bsz * t, HGRN_WIDTH), w_out_bf, xp.reshape(bsz * t, d), g0, b0,
                               TM_PROJ).reshape(bsz, t, d)
            proj_s = linear(xs, w_in_bf, sb, TN_LINEAR)
            y_s, s_s = hgrn_sample(proj_s, state_hgrn, j, hgrn_lb_logits, hgrn_norm_gain[j], layer)
            xs = proj_deepnorm(y_s, w_out_bf, xs, g0, b0, sb)
            hs_p.append(s_p)
            hs_s.append(s_s)
        g1, b1 = ln_gain[layer, 1], ln_bias[layer, 1]
        xp = ffn_deepnorm(xp.reshape(bsz * t, d), ffn_in_bf, ffn_out_bf, layer, g1, b1, TM_FFN, FFN_HC).reshape(bsz, t, d)
        xs = ffn_deepnorm(xs, ffn_in_bf, ffn_out_bf, layer, g1, b1, sb, FFN_HC)
    kvc, kvs, kvw = kv_pages
    n_page = t // PAGE_SIZE
    pages = lambda a: _token_major(a.reshape((n_nsa, bsz * n_page) + a.shape[3:]))
    win_rows = _token_major(kvw[:, :, n_page - wb_prompt // PAGE_SIZE:])
    win_p = win_rows.reshape((n_nsa, bsz, wb_prompt) + kv_shape)
    return (xp, xs.reshape(sb, 1, d), pages(kvc), jnp.stack(cmp_s), pages(kvs), jnp.stack(slc_s),
            win_p, jnp.stack(win_s), jnp.stack(gv_s), jnp.stack(hs_p), jnp.stack(hs_s))
```

```python
import functools

import jax
import jax.numpy as jnp
import numpy as np
from jax import lax
from jax.experimental import pallas as pl
from jax.experimental.pallas import tpu as pltpu

F32 = jnp.float32
BF16 = jnp.bfloat16

D_MODEL = 1024
DEPTH = 4
PAGE_SIZE = 128
N_MIXERS = 3

NSA_HEADS = 16
NSA_KV_HEADS = 4
NSA_GROUP = NSA_HEADS // NSA_KV_HEADS
HEAD_DIM = D_MODEL // NSA_HEADS
Q_WIDTH = NSA_HEADS * HEAD_DIM
KV_WIDTH = NSA_KV_HEADS * HEAD_DIM
CMP_LEN = 32
CMP_STRIDE = 16
SLC_BLOCK = 64
TOP_N = 16
WINDOW = 512
FORCE_BONUS = 1e3
ATTN_SCALE = HEAD_DIM ** -0.5
NEG_INF = -1e30

GMLP_WIDTH = D_MODEL
GMLP_GROUPS = 8
GMLP_GROUP_DIM = GMLP_WIDTH // GMLP_GROUPS
GMLP_CHUNK = 128

HGRN_HEADS = 8
HGRN_DK = 128
HGRN_DV = 128
HGRN_WIDTH = HGRN_HEADS * HGRN_DK

FFN_HIDDEN = 2816
DEEPNORM_ALPHA = (2 * DEPTH) ** 0.25
LN_EPS = 1e-5

LANES = 128
BF16_ROWS = 16
VMEM_LIMIT = 48 * 1024 * 1024


def _params(*sem):
    return pltpu.CompilerParams(dimension_semantics=sem, vmem_limit_bytes=VMEM_LIMIT)


def _dot(a, b):
    return jnp.dot(a, b, preferred_element_type=F32)


def _dot_nt(a, b):
    return lax.dot_general(a, b, (((1,), (1,)), ((), ())), preferred_element_type=F32)


def _layer_norm_rows(y, g, b):
    mu = jnp.mean(y, axis=-1, keepdims=True)
    yc = y - mu
    var = jnp.mean(yc * yc, axis=-1, keepdims=True)
    return yc * lax.rsqrt(var + LN_EPS) * g + b


def _deepnorm_rows(x, h, g, b):
    return _layer_norm_rows(DEEPNORM_ALPHA * x + h, g, b)


def _split3(x):
    hi = x.astype(BF16)
    r1 = x - hi.astype(F32)
    mid = r1.astype(BF16)
    lo = (r1 - mid.astype(F32)).astype(BF16)
    return hi, mid, lo


def _proj_deepnorm_kernel(a_ref, w_ref, x_ref, g_ref, b_ref, o_ref):
    h = _dot(a_ref[...].astype(BF16), w_ref[...])
    o_ref[...] = _deepnorm_rows(x_ref[...], h, g_ref[...], b_ref[...])


def proj_deepnorm(a, w_bf, x, g, b, tm):
    m, k = a.shape
    d = x.shape[1]
    tm = min(tm, m)
    return pl.pallas_call(
        _proj_deepnorm_kernel,
        grid=(m // tm,),
        in_specs=[pl.BlockSpec((tm, k), lambda i: (i, 0)),
                  pl.BlockSpec((k, d), lambda i: (0, 0)),
                  pl.BlockSpec((tm, d), lambda i: (i, 0)),
                  pl.BlockSpec((1, d), lambda i: (0, 0)),
                  pl.BlockSpec((1, d), lambda i: (0, 0))],
        out_specs=pl.BlockSpec((tm, d), lambda i: (i, 0)),
        out_shape=jax.ShapeDtypeStruct((m, d), F32),
        compiler_params=_params("parallel"),
        name="proj_deepnorm",
    )(a, w_bf, x, g.reshape(1, d), b.reshape(1, d))


def _ffn_kernel(x_ref, wg_ref, wu_ref, wo_ref, g_ref, b_ref, o_ref, xb_ref, acc_ref):
    j = pl.program_id(1)

    @pl.when(j == 0)
    def _():
        xb_ref[...] = x_ref[...].astype(BF16)
        acc_ref[...] = jnp.zeros_like(acc_ref)

    xb = xb_ref[...]
    gate = _dot(xb, wg_ref[...])
    up = _dot(xb, wu_ref[...])
    mid = (gate * jax.nn.sigmoid(gate) * up).astype(BF16)
    acc_ref[...] += _dot(mid, wo_ref[...])

    @pl.when(j == pl.num_programs(1) - 1)
    def _():
        o_ref[...] = _deepnorm_rows(x_ref[...], acc_ref[...], g_ref[...], b_ref[...])


def ffn_deepnorm(x, w_in_bf, w_out_bf, layer, g, b, tm, hc):
    m, d = x.shape
    hidden = w_out_bf.shape[1]
    tm = min(tm, m)
    nh = hidden // hc
    return pl.pallas_call(
        _ffn_kernel,
        grid=(m // tm, nh),
        in_specs=[pl.BlockSpec((tm, d), lambda i, j: (i, 0)),
                  pl.BlockSpec((None, d, hc), lambda i, j: (layer, 0, j)),
                  pl.BlockSpec((None, d, hc), lambda i, j: (layer, 0, nh + j)),
                  pl.BlockSpec((None, hc, d), lambda i, j: (layer, j, 0)),
                  pl.BlockSpec((1, d), lambda i, j: (0, 0)),
                  pl.BlockSpec((1, d), lambda i, j: (0, 0))],
        out_specs=pl.BlockSpec((tm, d), lambda i, j: (i, 0)),
        out_shape=jax.ShapeDtypeStruct((m, d), F32),
        scratch_shapes=[pltpu.VMEM((tm, d), BF16), pltpu.VMEM((tm, d), F32)],
        compiler_params=_params("parallel", "arbitrary"),
        name="ffn_deepnorm",
    )(x, w_in_bf, w_in_bf, w_out_bf, g.reshape(1, d), b.reshape(1, d))


def _gelu_exact(z):
    return 0.5 * z * (1.0 + lax.erf(z * (0.5 ** 0.5)))


def _gmlp_kernel(x_ref, win_ref, bin_ref, lnv_ref, wsp_ref, bsp_ref, wout_ref, g_ref, b_ref,
                 o_ref, *rest, chunked):
    width = GMLP_WIDTH
    x = x_ref[...]
    z = _dot(x.astype(BF16), win_ref[...]) + bin_ref[...]
    gz = _gelu_exact(z)
    u = gz[:, :width]
    v = _layer_norm_rows(gz[:, width:], lnv_ref[0:1, :], lnv_ref[1:2, :])
    if chunked:
        mix_ref, = rest
        vb = v.astype(BF16)
        for c in range(x.shape[0] // GMLP_CHUNK):
            rows = slice(c * GMLP_CHUNK, (c + 1) * GMLP_CHUNK)
            for h in range(GMLP_GROUPS):
                cols = slice(h * GMLP_GROUP_DIM, (h + 1) * GMLP_GROUP_DIM)
                mix_ref[rows, cols] = _dot(wsp_ref[h], vb[rows, cols]) + bsp_ref[:, cols]
        mixed = mix_ref[...]
    else:
        v_ref, = rest
        v_ref[...] = v
        mixed = v * wsp_ref[...] + bsp_ref[...]
    h_out = _dot((u * mixed).astype(BF16), wout_ref[...])
    o_ref[...] = _deepnorm_rows(x, h_out, g_ref[...], b_ref[...])


def gmlp_layer(x, w_in_bf, b_in, ln_v, w_sp, b_sp, w_out_bf, g, b, *, chunked, tm):
    m, d = x.shape
    width = GMLP_WIDTH
    tm = min(tm, m)
    const = lambda *shape: pl.BlockSpec(shape, lambda i: (0,) * len(shape))
    if chunked:
        wsp = jnp.tril(w_sp).astype(BF16)
        bsp = jnp.repeat(b_sp.T, GMLP_GROUP_DIM, axis=1)
        wsp_spec, bsp_spec = const(GMLP_GROUPS, GMLP_CHUNK, GMLP_CHUNK), const(GMLP_CHUNK, width)
        out_shape = jax.ShapeDtypeStruct((m, d), F32)
        out_specs = pl.BlockSpec((tm, d), lambda i: (i, 0))
        scratch = [pltpu.VMEM((tm, width), F32)]
    else:
        wsp = jnp.repeat(w_sp[:, 0, 0], GMLP_GROUP_DIM).reshape(1, width)
        bsp = jnp.repeat(b_sp[:, 0], GMLP_GROUP_DIM).reshape(1, width)
        wsp_spec, bsp_spec = const(1, width), const(1, width)
        out_shape = (jax.ShapeDtypeStruct((m, d), F32), jax.ShapeDtypeStruct((m, width), F32))
        out_specs = (pl.BlockSpec((tm, d), lambda i: (i, 0)), pl.BlockSpec((tm, width), lambda i: (i, 0)))
        scratch = []
    return pl.pallas_call(
        functools.partial(_gmlp_kernel, chunked=chunked),
        grid=(m // tm,),
        in_specs=[pl.BlockSpec((tm, d), lambda i: (i, 0)),
                  const(d, 2 * width), const(1, 2 * width), const(2, width),
                  wsp_spec, bsp_spec, const(width, d), const(1, d), const(1, d)],
        out_specs=out_specs,
        out_shape=out_shape,
        scratch_shapes=scratch,
        compiler_params=_params("parallel"),
        name="gmlp_layer",
    )(x, w_in_bf, b_in.reshape(1, 2 * width), ln_v, wsp, bsp, w_out_bf, g.reshape(1, d), b.reshape(1, d))


def _linear_kernel(x_ref, w_ref, o_ref):
    o_ref[...] = _dot(x_ref[...].astype(BF16), w_ref[...])


def linear(x, w_bf, tm, tn):
    m, k = x.shape
    n = w_bf.shape[1]
    tm, tn = min(tm, m), min(tn, n)
    return pl.pallas_call(
        _linear_kernel,
        grid=(m // tm, n // tn),
        in_specs=[pl.BlockSpec((tm, k), lambda i, j: (i, 0)),
                  pl.BlockSpec((k, tn), lambda i, j: (0, j))],
        out_specs=pl.BlockSpec((tm, tn), lambda i, j: (i, j)),
        out_shape=jax.ShapeDtypeStruct((m, n), F32),
        compiler_params=_params("parallel", "parallel"),
        name="linear",
    )(x, w_bf)


def _token_minor(cache):
    nd = cache.ndim
    return jnp.transpose(cache, tuple(range(nd - 4)) + (nd - 3, nd - 2, nd - 1, nd - 4))


def _token_major(kvt):
    nd = kvt.ndim
    return jnp.transpose(kvt, tuple(range(nd - 4)) + (nd - 1, nd - 4, nd - 3, nd - 2))


CMP_SLABS = 2 * KV_WIDTH // LANES


def _cmp_weights(w_cmp):
    w = w_cmp.reshape(2, 2, CMP_STRIDE, HEAD_DIM, HEAD_DIM)
    eye = jnp.eye(NSA_KV_HEADS, dtype=w_cmp.dtype)
    wbd = jnp.einsum('gh,klpde->pkgdlhe', eye, w)
    return wbd.reshape(CMP_STRIDE, 2, KV_WIDTH, 2 * KV_WIDTH).astype(BF16)


def _cmp_operands(w_cmp, pe_cmp, k_natural):
    wbd = _cmp_weights(w_cmp)
    wk = wbd[:, 0] if k_natural else jnp.swapaxes(wbd[:, 0], 1, 2)
    wv = jnp.swapaxes(wbd[:, 1], 1, 2)
    pe = pe_cmp.reshape(2, CMP_LEN * HEAD_DIM)
    wflat = w_cmp.reshape(2, CMP_LEN * HEAD_DIM, HEAD_DIM).astype(BF16)
    wflat_t = jnp.swapaxes(wflat, 1, 2)
    const = lambda a: pl.BlockSpec(a.shape, lambda *_: (0,) * a.ndim)
    ops = (wk, wv, pe, wflat, wflat_t)
    return ops, [const(a) for a in ops]


def _compress(load_rows, nhb, period, n_cmp, wk_ref, wv_ref, pe_ref, wflat_ref, wflat_t_ref, k_natural):
    kw = KV_WIDTH
    acc_k = jnp.zeros((nhb, 2 * kw) if k_natural else (2 * kw, nhb), F32)
    acc_v = jnp.zeros((2 * kw, nhb), F32)
    for p in range(CMP_STRIDE):
        slabs = [load_rows(p, s).astype(BF16) for s in range(CMP_SLABS)]
        xk = jnp.concatenate(slabs[:CMP_SLABS // 2], axis=1)
        xv = jnp.concatenate(slabs[CMP_SLABS // 2:], axis=1)
        acc_k = acc_k + (_dot(xk, wk_ref[p]) if k_natural else _dot_nt(wk_ref[p], xk))
        acc_v = acc_v + _dot_nt(wv_ref[p], xv)

    def pe8(kv):
        return jnp.broadcast_to(pe_ref[kv:kv + 1, :], (8, CMP_LEN * HEAD_DIM)).astype(BF16)

    def finish_t(acc, kv):
        bias = _dot_nt(wflat_t_ref[kv], pe8(kv))[:, 0:1]
        bias = jnp.concatenate([bias] * NSA_KV_HEADS, axis=0)
        out = acc[:kw, :] + pltpu.roll(acc[kw:, :], nhb - 1, 1) + bias
        return jnp.where(lax.broadcasted_iota(jnp.int32, (1, nhb), 1) % period < n_cmp, out, 0.0)

    if k_natural:
        bias = _dot(pe8(0), wflat_ref[0])[0:1, :]
        bias = jnp.concatenate([bias] * NSA_KV_HEADS, axis=1)
        k_out = acc_k[:, :kw] + pltpu.roll(acc_k[:, kw:], nhb - 1, 0) + bias
        k_out = jnp.where(lax.broadcasted_iota(jnp.int32, (nhb, 1), 0) % period < n_cmp, k_out, 0.0)
    else:
        k_out = finish_t(acc_k, 0)
    return k_out, finish_t(acc_v, 1)


def _cmap_t(n_cmp, n_slc, rows, cols):
    c0 = np.arange(cols)[None, :] * CMP_STRIDE
    s0 = np.arange(rows)[:, None] * SLC_BLOCK
    ov = np.minimum(c0 + CMP_LEN, s0 + SLC_BLOCK) - np.maximum(c0, s0)
    ov = np.clip(ov, 0, None).astype(np.float32) / CMP_STRIDE
    keep = (np.arange(cols)[None, :] < n_cmp) & (np.arange(rows)[:, None] < n_slc)
    return np.where(keep, ov, 0.0).astype(np.float32)


def _rank_before(score, n_slc, axis, blk):
    rank = jnp.zeros(score.shape, jnp.int32)
    for j in range(n_slc):
        other = score[j:j + 1, :] if axis == 0 else score[:, j:j + 1]
        before = (other > score) | ((other == score) & (j < blk))
        rank = rank + before.astype(jnp.int32)
    return rank


NSA_QB = 128
NSA_KT = 2 * NSA_QB
NSA_GATE_ROWS = 64
NSA_T_ROWS = Q_WIDTH + 6 * KV_WIDTH + NSA_GATE_ROWS
NSA_NAT_COLS = 4 * KV_WIDTH


def _nsa_proj_kernel(x_ref, wn_ref, wt_ref, bg_ref, *rest, n_prev):
    xc_ref, ks_ref, kw_ref, qt_ref, vst_ref, vwt_ref, gt_ref, kvc_ref, kvs_ref, kvw_ref = rest[n_prev:]
    tq = x_ref.shape[1]
    kw_, hd = KV_WIDTH, HEAD_DIM
    xb = x_ref[0].astype(BF16)
    nat = _dot(xb, wn_ref[...])
    for s in range(CMP_SLABS):
        xc_ref[0, s] = nat[:, s * LANES:(s + 1) * LANES]
    tok = pl.program_id(1) * tq + lax.broadcasted_iota(jnp.int32, (tq, hd), 0)
    onehot = (tok // SLC_BLOCK == lax.broadcasted_iota(jnp.int32, (tq, hd), 1)).astype(BF16)
    for g in range(NSA_KV_HEADS):
        k_slc = nat[:, 2 * kw_ + g * hd:2 * kw_ + (g + 1) * hd].astype(BF16)
        ks_ref[0, g] = jnp.concatenate([k_slc, onehot], axis=1)
        kw_ref[0, g] = nat[:, 3 * kw_ + g * hd:3 * kw_ + (g + 1) * hd].astype(BF16)
    t = _dot_nt(wt_ref[...], xb)
    kv0 = Q_WIDTH
    for g in range(NSA_KV_HEADS):
        r_slc = kv0 + 3 * kw_ + g * hd
        r_win = kv0 + 5 * kw_ + g * hd
        vst_ref[0, g] = t[r_slc:r_slc + hd, :].astype(BF16)
        vwt_ref[0, g] = t[r_win:r_win + hd, :].astype(BF16)
    gates = jax.nn.sigmoid(t[kv0 + 6 * kw_:, :] + bg_ref[...])
    for nb in range(tq // NSA_QB):
        cols = slice(nb * NSA_QB, (nb + 1) * NSA_QB)
        gt_ref[0, nb] = gates[:, cols]
        for br, ref in enumerate((kvc_ref, kvs_ref, kvw_ref)):
            rows = t[kv0 + br * 2 * kw_:kv0 + (br + 1) * 2 * kw_, cols]
            ref[0, nb] = rows.reshape(2, NSA_KV_HEADS, hd, NSA_QB)
        for g in range(NSA_KV_HEADS):
            for r in range(NSA_GROUP):
                r0 = (g * NSA_GROUP + r) * hd
                qt_ref[0, nb, g, :, r * NSA_QB:(r + 1) * NSA_QB] = (
                    t[r0:r0 + hd, cols] * ATTN_SCALE).astype(BF16)


def nsa_prompt_proj(x, w_in, b_gate, tq, layer, n_layers, prev_pages):
    bsz, t, d = x.shape
    assert NSA_QB == PAGE_SIZE and t // SLC_BLOCK <= HEAD_DIM
    nb, nbk = t // NSA_QB, tq // NSA_QB
    g_, hd, w2 = NSA_KV_HEADS, HEAD_DIM, 2 * KV_WIDTH
    w_q = w_in[:, :Q_WIDTH]
    w_kv = w_in[:, Q_WIDTH:Q_WIDTH + 3 * w2]
    w_g = w_in[:, Q_WIDTH + 3 * w2:]
    pad = jnp.zeros((d, NSA_GATE_ROWS - 3 * NSA_HEADS), w_in.dtype)
    w_t = jnp.concatenate([w_q, w_kv, w_g, pad], axis=1).T.astype(BF16)
    w_nat = jnp.concatenate([w_kv[:, :w2], w_kv[:, w2:w2 + KV_WIDTH], w_kv[:, 2 * w2:2 * w2 + KV_WIDTH]],
                            axis=1).astype(BF16)
    bg = jnp.concatenate([b_gate, jnp.zeros((NSA_GATE_ROWS - 3 * NSA_HEADS,), b_gate.dtype)]).reshape(NSA_GATE_ROWS, 1)
    page = jax.ShapeDtypeStruct((n_layers, bsz, nb, 2, g_, hd, NSA_QB), F32)
    page_spec = pl.BlockSpec((None, 1, nbk, 2, g_, hd, NSA_QB), lambda b, i: (layer, b, i, 0, 0, 0, 0))
    out_shape = (
        jax.ShapeDtypeStruct((bsz, CMP_SLABS, t, LANES), F32),
        jax.ShapeDtypeStruct((bsz, g_, t, 2 * hd), BF16), jax.ShapeDtypeStruct((bsz, g_, t, hd), BF16),
        jax.ShapeDtypeStruct((bsz, nb, g_, hd, NSA_GROUP * NSA_QB), BF16),
        jax.ShapeDtypeStruct((bsz, g_, hd, t), BF16), jax.ShapeDtypeStruct((bsz, g_, hd, t), BF16),
        jax.ShapeDtypeStruct((bsz, nb, NSA_GATE_ROWS, NSA_QB), F32),
        page, page, page,
    )
    out_specs = (
        pl.BlockSpec((1, CMP_SLABS, tq, LANES), lambda b, i: (b, 0, i, 0)),
        pl.BlockSpec((1, g_, tq, 2 * hd), lambda b, i: (b, 0, i, 0)),
        pl.BlockSpec((1, g_, tq, hd), lambda b, i: (b, 0, i, 0)),
        pl.BlockSpec((1, nbk, g_, hd, NSA_GROUP * NSA_QB), lambda b, i: (b, i, 0, 0, 0)),
        pl.BlockSpec((1, g_, hd, tq), lambda b, i: (b, 0, 0, i)),
        pl.BlockSpec((1, g_, hd, tq), lambda b, i: (b, 0, 0, i)),
        pl.BlockSpec((1, nbk, NSA_GATE_ROWS, NSA_QB), lambda b, i: (b, i, 0, 0)),
        page_spec, page_spec, page_spec,
    )
    prev = tuple(prev_pages) if prev_pages is not None else ()
    n_out = len(out_shape)
    aliases = {4 + i: n_out - len(prev) + i for i in range(len(prev))}
    return pl.pallas_call(
        functools.partial(_nsa_proj_kernel, n_prev=len(prev)),
        grid=(bsz, t // tq),
        in_specs=[pl.BlockSpec((1, tq, d), lambda b, i: (b, i, 0)),
                  pl.BlockSpec((d, NSA_NAT_COLS), lambda b, i: (0, 0)),
                  pl.BlockSpec((NSA_T_ROWS, d), lambda b, i: (0, 0)),
                  pl.BlockSpec((NSA_GATE_ROWS, 1), lambda b, i: (0, 0))]
                 + [pl.BlockSpec(memory_space=pl.ANY)] * len(prev),
        out_specs=out_specs,
        out_shape=out_shape,
        input_output_aliases=aliases,
        compiler_params=_params("parallel", "parallel"),
        name="nsa_prompt_proj",
    )(x, w_nat, w_t, bg, *prev)


def _nsa_compress_prompt_kernel(x_ref, wk_ref, wv_ref, pe_ref, wflat_ref, wflat_t_ref, kc_ref, vct_ref, *, n_cmp):
    nhb = x_ref.shape[2] // CMP_STRIDE
    load = lambda p, s: x_ref[0, s, pl.ds(p, nhb, stride=CMP_STRIDE), :]
    k_all, v_t = _compress(load, nhb, nhb, n_cmp, wk_ref, wv_ref, pe_ref, wflat_ref, wflat_t_ref, True)
    for g in range(NSA_KV_HEADS):
        kc_ref[0, g] = k_all[:, g * HEAD_DIM:(g + 1) * HEAD_DIM].astype(BF16)
        vct_ref[0, g] = v_t[g * HEAD_DIM:(g + 1) * HEAD_DIM, :].astype(BF16)


def nsa_compress_prompt(kv_cmp, w_cmp, pe_cmp):
    bsz, _, t, _ = kv_cmp.shape
    nhb = t // CMP_STRIDE
    n_cmp = (t - CMP_LEN) // CMP_STRIDE + 1
    ops, specs = _cmp_operands(w_cmp, pe_cmp, True)
    return pl.pallas_call(
        functools.partial(_nsa_compress_prompt_kernel, n_cmp=n_cmp),
        grid=(bsz,),
        in_specs=[pl.BlockSpec((1, CMP_SLABS, t, LANES), lambda b: (b, 0, 0, 0))] + specs,
        out_specs=(pl.BlockSpec((1, NSA_KV_HEADS, nhb, HEAD_DIM), lambda b: (b, 0, 0, 0)),
                   pl.BlockSpec((1, NSA_KV_HEADS, HEAD_DIM, nhb), lambda b: (b, 0, 0, 0))),
        out_shape=(jax.ShapeDtypeStruct((bsz, NSA_KV_HEADS, nhb, HEAD_DIM), BF16),
                   jax.ShapeDtypeStruct((bsz, NSA_KV_HEADS, HEAD_DIM, nhb), BF16)),
        compiler_params=_params("parallel"),
        name="nsa_compress_prompt",
    )(kv_cmp, *ops)


def _attn_bias_consts():
    k = np.arange(NSA_QB)[:, None]
    q = np.arange(NSA_QB)[None, :]
    x4 = lambda a: np.tile(a.astype(np.float32), (1, NSA_GROUP))
    causal = x4(np.where(k <= q, 0.0, NEG_INF))
    lower = x4(np.where(k > q, 0.0, NEG_INF))
    zeros, neg = np.zeros_like(causal), np.full_like(causal, NEG_INF)
    win_bias = np.stack([zeros, causal, neg, lower])
    slc_bias = np.stack([np.concatenate([causal, neg]), np.concatenate([zeros, causal])])
    return win_bias, slc_bias


WB_ZERO, WB_CAUSAL, WB_NEG, WB_LOWER = 0, 1, 2, 3


def _softmax_cols(s, mask):
    sm = jnp.where(mask, s, NEG_INF)
    m = jnp.max(sm, axis=0, keepdims=True)
    e = jnp.where(mask, jnp.exp(sm - m), 0.0)
    l = jnp.sum(e, axis=0, keepdims=True)
    return e / jnp.where(l > 0.0, l, 1.0)


def _lanes_x4(a):
    return jnp.concatenate([a] * NSA_GROUP, axis=1)


def _softmax_pv(score_tile, n_tiles, tile, s_ref, p_ref, vt):
    tiles = [slice(j * tile, (j + 1) * tile) for j in range(n_tiles)]
    n_rows = n_tiles * tile
    m = None
    for j, rows in enumerate(tiles):
        s = score_tile(j)
        s_ref[rows, :] = s
        m_j = jnp.max(s, axis=0, keepdims=True)
        m = m_j if m is None else jnp.maximum(m, m_j)
    l = jnp.zeros_like(m)
    for rows in tiles:
        e = jnp.exp(s_ref[rows, :] - m)
        l = l + jnp.sum(e, axis=0, keepdims=True)
        p_ref[rows, :] = e.astype(BF16)
    return _dot(vt, p_ref[0:n_rows, :]) / l


def _nsa_prompt_attn_kernel(qt_ref, gt_ref, kc_ref, vct_ref, ks_ref, vst_ref, kw_ref, vwt_ref, cmap_ref,
                            wbias_ref, sbias_ref, o_ref, s_ref, p_ref, oslc_ref, *, n_cmp, n_slc, nb):
    qb, kt = NSA_QB, NSA_KT
    width = NSA_GROUP * qb
    g = pl.program_id(1)
    n = pl.program_id(2)
    s0 = n * qb
    qt = qt_ref[0, 0, 0]
    qpos1 = s0 + lax.broadcasted_iota(jnp.int32, (1, qb), 1)
    qpos = _lanes_x4(qpos1)

    ncp = kc_ref.shape[2]
    crow = lax.broadcasted_iota(jnp.int32, (ncp, 1), 0)
    cmask = (crow * CMP_STRIDE + (CMP_LEN - 1) <= qpos) & (crow < n_cmp)
    p_cmp = _softmax_cols(_dot(kc_ref[0, 0], qt), cmask)
    o_cmp = _dot(vct_ref[0, 0], p_cmp.astype(BF16))

    psum = p_cmp[:, 0:qb]
    for r in range(1, NSA_GROUP):
        psum = psum + p_cmp[:, r * qb:(r + 1) * qb]
    imp = sum(_dot(cmap_ref[...], piece) for piece in _split3(psum))
    nsp = cmap_ref.shape[0]
    blk = lax.broadcasted_iota(jnp.int32, (nsp, 1), 0)
    cur = qpos1 // SLC_BLOCK
    forced = (blk == 0) | (blk == cur) | (blk == cur - 1)
    allowed = (blk <= cur) & (blk < n_slc)
    score = jnp.where(allowed, imp + jnp.where(forced, FORCE_BONUS, 0.0), -jnp.inf)
    sel = (_rank_before(score, n_slc, 0, blk) < min(TOP_N, n_slc)) & allowed
    sel_bias = _lanes_x4(jnp.where(sel, 0.0, NEG_INF)).astype(BF16)
    q_ext = jnp.concatenate([qt, sel_bias, jnp.zeros((HEAD_DIM - nsp, width), BF16)], axis=0)

    last = n // (kt // qb)
    for k_last in range(nb // (kt // qb)):
        @pl.when(last == k_last)
        def _():
            def score_tile(j):
                s = _dot(ks_ref[0, 0, j * kt:(j + 1) * kt, :], q_ext)
                return s + sbias_ref[n % (kt // qb)] if j == k_last else s

            oslc_ref[...] = _softmax_pv(score_tile, k_last + 1, kt, s_ref, p_ref,
                                        vst_ref[0, 0, :, 0:(k_last + 1) * kt])

    o_slc = oslc_ref[...]

    n_win = WINDOW // qb
    n_wt = min(n_win + 1, nb)
    first = jnp.clip(n - (n_wt - 1), 0, nb - n_wt)
    def win_tile(i):
        j = first + i
        which = jnp.where(j > n, WB_NEG, jnp.where(j == n, WB_CAUSAL, jnp.where(j == n - n_win, WB_LOWER, WB_ZERO)))
        rows = pl.ds(pl.multiple_of(j * qb, qb), qb)
        return _dot(kw_ref[0, 0, rows, :], qt) + wbias_ref[which]

    rows = pl.ds(pl.multiple_of(first * qb, qb), n_wt * qb)
    o_win = _softmax_pv(win_tile, n_wt, qb, s_ref, p_ref, vwt_ref[0, 0, :, rows])

    def gate(branch):
        rows = gt_ref[0, 0, pl.ds(branch * NSA_HEADS + g * NSA_GROUP, NSA_GROUP), :]
        return jnp.concatenate([rows[r:r + 1, :] for r in range(NSA_GROUP)], axis=1)

    o = gate(0) * o_cmp + gate(1) * o_slc + gate(2) * o_win
    for r in range(NSA_GROUP):
        o_ref[0, 0, r * HEAD_DIM:(r + 1) * HEAD_DIM, :] = o[:, r * qb:(r + 1) * qb]


def nsa_prompt_attention(qt, gt, kc, vct, ks, vst, kw, vwt):
    bsz, nb, g_, hd, width = qt.shape
    t = kw.shape[2]
    assert t % NSA_KT == 0
    ncp = kc.shape[2]
    n_cmp = (t - CMP_LEN) // CMP_STRIDE + 1
    n_slc = -(-t // SLC_BLOCK)
    nsp = -(-n_slc // BF16_ROWS) * BF16_ROWS
    cmap = jnp.asarray(_cmap_t(n_cmp, n_slc, nsp, ncp), BF16)
    win_bias, slc_bias = _attn_bias_consts()
    per_bg = lambda *shape: pl.BlockSpec((1, 1) + shape, lambda b, g, n: (b, g, 0, 0))
    const = lambda a: pl.BlockSpec(a.shape, lambda b, g, n: (0,) * a.ndim)
    return pl.pallas_call(
        functools.partial(_nsa_prompt_attn_kernel, n_cmp=n_cmp, n_slc=n_slc, nb=nb),
        grid=(bsz, g_, nb),
        in_specs=[pl.BlockSpec((1, 1, 1, hd, width), lambda b, g, n: (b, n, g, 0, 0)),
                  pl.BlockSpec((1, 1, NSA_GATE_ROWS, NSA_QB), lambda b, g, n: (b, n, 0, 0)),
                  per_bg(ncp, hd), per_bg(hd, ncp),
                  per_bg(t, 2 * hd), per_bg(hd, t), per_bg(t, hd), per_bg(hd, t),
                  const(cmap), const(win_bias), const(slc_bias)],
        out_specs=pl.BlockSpec((1, 1, NSA_GROUP * hd, NSA_QB), lambda b, g, n: (b, n, g, 0)),
        out_shape=jax.ShapeDtypeStruct((bsz, nb, Q_WIDTH, NSA_QB), F32),
        scratch_shapes=[pltpu.VMEM((t, width), F32), pltpu.VMEM((t, width), BF16), pltpu.VMEM((hd, width), F32)],
        compiler_params=_params("parallel", "parallel", "arbitrary"),
        name="nsa_prompt_attention",
    )(qt, gt, kc, vct, ks, vst, kw, vwt, cmap, win_bias, slc_bias)


def _proj_t_deepnorm_kernel(at_ref, w_ref, x_ref, g_ref, b_ref, o_ref):
    a = at_ref[0, 0].T.astype(BF16)
    o_ref[0] = _deepnorm_rows(x_ref[0], _dot(a, w_ref[...]), g_ref[...], b_ref[...])


def proj_t_deepnorm(a_t, w_bf, x, g, b):
    bsz, nb, k, qb = a_t.shape
    d = x.shape[2]
    return pl.pallas_call(
        _proj_t_deepnorm_kernel,
        grid=(bsz, nb),
        in_specs=[pl.BlockSpec((1, 1, k, qb), lambda b, n: (b, n, 0, 0)),
                  pl.BlockSpec((k, d), lambda b, n: (0, 0)),
                  pl.BlockSpec((1, qb, d), lambda b, n: (b, n, 0)),
                  pl.BlockSpec((1, d), lambda b, n: (0, 0)),
                  pl.BlockSpec((1, d), lambda b, n: (0, 0))],
        out_specs=pl.BlockSpec((1, qb, d), lambda b, n: (b, n, 0)),
        out_shape=jax.ShapeDtypeStruct(x.shape, F32),
        compiler_params=_params("parallel", "parallel"),
        name="proj_t_deepnorm",
    )(a_t, w_bf, x, g.reshape(1, d), b.reshape(1, d))


def nsa_prompt_layer(x, w_in, b_gate, w_cmp, pe_cmp, w_out, g, b, layer, n_layers, prev_pages):
    xc, ks, kw, qt, vst, vwt, gt, kvc, kvs, kvw = nsa_prompt_proj(x, w_in, b_gate, 512, layer, n_layers, prev_pages)
    kc, vct = nsa_compress_prompt(xc, w_cmp, pe_cmp)
    o_t = nsa_prompt_attention(qt, gt, kc, vct, ks, vst, kw, vwt)
    return proj_t_deepnorm(o_t, w_out.astype(BF16), x, g, b), (kvc, kvs, kvw)


SAMPLE_SEQS_PER_STEP = 2


def _page_specs(cache_t, layer, n_seq, n_pages):
    block = (1, 1) + cache_t.shape[2:]
    return [pl.BlockSpec(block, lambda b, pt, si=si, k=k: (layer, pt[b * n_seq + si, k], 0, 0, 0, 0))
            for si in range(n_seq) for k in range(n_pages)]


def _nsa_compress_sample_kernel(*refs, n_seq, n_pages, n_cmp):
    page_refs = refs[1:1 + n_seq * n_pages]
    wk_ref, wv_ref, pe_ref, wflat_ref, wflat_t_ref, perm_ref, kct_ref, vct_ref, rows_ref = refs[1 + n_seq * n_pages:]
    kw = KV_WIDTH
    hb_page = PAGE_SIZE // CMP_STRIDE
    for k in range(n_seq * n_pages):
        for kv in range(2):
            tile = page_refs[k][0, 0, kv].reshape(kw, PAGE_SIZE).astype(BF16)
            by_p = _dot(tile, perm_ref[...]).T
            for p in range(CMP_STRIDE):
                rows_ref[kv, p, k * hb_page:(k + 1) * hb_page, :] = by_p[p * hb_page:(p + 1) * hb_page, :]
    period = n_pages * hb_page
    slabs_kv = kw // LANES
    load = lambda p, s: rows_ref[s // slabs_kv, p, :, (s % slabs_kv) * LANES:(s % slabs_kv + 1) * LANES]
    k_t, v_t = _compress(load, n_seq * period, period, n_cmp, wk_ref, wv_ref, pe_ref, wflat_ref, wflat_t_ref, False)
    for si in range(n_seq):
        kct_ref[si] = k_t[:, si * period:(si + 1) * period].astype(BF16)
        vct_ref[si] = v_t[:, si * period:(si + 1) * period].astype(BF16)


def nsa_compress_sample(cache_cmp_t, layer, page_table, w_cmp, pe_cmp):
    bsz, n_pages = page_table.shape
    nhb = n_pages * PAGE_SIZE // CMP_STRIDE
    n_cmp = (n_pages * PAGE_SIZE - CMP_LEN) // CMP_STRIDE + 1
    ops, specs = _cmp_operands(w_cmp, pe_cmp, False)
    tok = np.arange(PAGE_SIZE)
    perm = np.zeros((PAGE_SIZE, PAGE_SIZE), np.float32)
    perm[tok, (tok % CMP_STRIDE) * (PAGE_SIZE // CMP_STRIDE) + tok // CMP_STRIDE] = 1.0
    ops = ops + (jnp.asarray(perm, BF16),)
    specs = specs + [pl.BlockSpec(perm.shape, lambda *_: (0, 0))]
    n_seq = SAMPLE_SEQS_PER_STEP
    page_specs = _page_specs(cache_cmp_t, layer, n_seq, n_pages)
    out = jax.ShapeDtypeStruct((bsz, KV_WIDTH, nhb), BF16)
    out_spec = pl.BlockSpec((n_seq, KV_WIDTH, nhb), lambda b, pt: (b, 0, 0))
    return pl.pallas_call(
        functools.partial(_nsa_compress_sample_kernel, n_seq=n_seq, n_pages=n_pages, n_cmp=n_cmp),
        grid_spec=pltpu.PrefetchScalarGridSpec(
            num_scalar_prefetch=1, grid=(bsz // n_seq,),
            in_specs=page_specs + specs, out_specs=(out_spec, out_spec),
            scratch_shapes=[pltpu.VMEM((2, CMP_STRIDE, n_seq * nhb, KV_WIDTH), F32)]),
        out_shape=(out, out),
        compiler_params=_params("parallel"),
        name="nsa_compress_sample",
    )(page_table, *([cache_cmp_t] * (n_seq * n_pages)), *ops)


def _nsa_sample_attn_kernel(pt_ref, q_ref, kvn_ref, gp_ref, bg_ref, kct_ref, vct_ref, win_ref,
                            cmap_ref, g16_ref, e_ref, *rest, past, n_cmp, n_slc):
    del pt_ref
    all_pages, o_ref = rest[:-1], rest[-1]
    n_seq = q_ref.shape[0]
    n_pages = len(all_pages) // n_seq
    for si in range(n_seq):
        _nsa_sample_attn_one(si, q_ref, kvn_ref, gp_ref, bg_ref, kct_ref, vct_ref, win_ref, cmap_ref, g16_ref,
                             e_ref, all_pages[si * n_pages:(si + 1) * n_pages], o_ref, past, n_cmp, n_slc)


def _nsa_sample_attn_one(si, q_ref, kvn_ref, gp_ref, bg_ref, kct_ref, vct_ref, win_ref, cmap_ref, g16_ref,
                         e_ref, page_refs, o_ref, past, n_cmp, n_slc):
    kw = KV_WIDTH
    q16 = q_ref[si] * ATTN_SCALE
    rowg = lax.broadcasted_iota(jnp.int32, (NSA_HEADS, 1), 0) // NSA_GROUP
    colg = lax.broadcasted_iota(jnp.int32, (1, kw), 1) // HEAD_DIM
    own = rowg == colg
    qbd = jnp.where(own, jnp.concatenate([q16] * NSA_KV_HEADS, axis=1), 0.0).astype(BF16)

    def pick(o_full):
        om = jnp.where(own, o_full, 0.0)
        out = om[:, 0:HEAD_DIM]
        for g in range(1, NSA_KV_HEADS):
            out = out + om[:, g * HEAD_DIM:(g + 1) * HEAD_DIM]
        return out

    def new_row(lo):
        return kvn_ref[si][:, lo:lo + kw]

    def attend(k_t, v_t, mask, k_new, v_new, new_ok):
        s = jnp.where(mask, _dot(qbd, k_t), NEG_INF)
        m = jnp.max(s, axis=1, keepdims=True)
        if k_new is not None:
            k8 = jnp.broadcast_to(k_new.astype(BF16), (8, kw))
            s_new = jnp.where(new_ok, _dot_nt(qbd, k8)[:, 0:1], NEG_INF)
            m = jnp.maximum(m, s_new)
        e = jnp.where(mask, jnp.exp(s - m), 0.0)
        l = jnp.sum(e, axis=1, keepdims=True)
        o_full = _dot_nt(e.astype(BF16), v_t)
        if k_new is not None:
            e_new = jnp.where(new_ok, jnp.exp(s_new - m), 0.0)
            l = l + e_new
            o_full = o_full + e_new * v_new
        l = jnp.where(l > 0.0, l, 1.0)
        return pick(o_full) / l, e / l

    q_pos = past
    ncp = kct_ref.shape[2]
    c = lax.broadcasted_iota(jnp.int32, (1, ncp), 1)
    cmask = (c * CMP_STRIDE + (CMP_LEN - 1) <= q_pos) & (c < n_cmp)
    o_cmp, p_cmp = attend(kct_ref[si], vct_ref[si], cmask, None, None, None)

    psum = sum(_dot(g16_ref[...], piece) for piece in _split3(p_cmp))
    imp = sum(_dot(piece, cmap_ref[...]) for piece in _split3(psum))
    nsp = cmap_ref.shape[1]
    blk = lax.broadcasted_iota(jnp.int32, (1, nsp), 1)
    cur = q_pos // SLC_BLOCK
    forced = (blk == 0) | (blk == cur) | (blk == cur - 1)
    allowed = (blk <= cur) & (blk < n_slc)
    score = jnp.where(allowed, imp + jnp.where(forced, FORCE_BONUS, 0.0), -jnp.inf)
    sel = (_rank_before(score, n_slc, 1, blk) < min(TOP_N, n_slc)) & allowed
    sel_keys = _dot(sel.astype(BF16), e_ref[...]) > 0.5

    k_t = jnp.concatenate([r[0, 0, 0].reshape(kw, PAGE_SIZE) for r in page_refs], axis=1).astype(BF16)
    v_t = jnp.concatenate([r[0, 0, 1].reshape(kw, PAGE_SIZE) for r in page_refs], axis=1).astype(BF16)
    new_ok = sel[:, cur:cur + 1]
    o_slc, _ = attend(k_t, v_t, sel_keys, new_row(2 * kw), new_row(3 * kw), new_ok)

    wb = win_ref.shape[-1]
    wpos = past - wb + lax.broadcasted_iota(jnp.int32, (1, wb), 1)
    wmask = (wpos <= q_pos) & (wpos > q_pos - WINDOW) & (wpos >= 0)
    o_win, _ = attend(win_ref[0, si, 0].reshape(kw, wb).astype(BF16), win_ref[0, si, 1].reshape(kw, wb).astype(BF16),
                      wmask, new_row(4 * kw), new_row(5 * kw), True)

    gates = jax.nn.sigmoid(gp_ref[si] + bg_ref[...])
    o_ref[si] = gates[:, 0:1] * o_cmp + gates[:, 1:2] * o_slc + gates[:, 2:3] * o_win


def nsa_sample_layer(x, cache_cmp_t, cache_slc_t, cache_win_t, layer, page_table, w_in, b_gate, w_cmp, pe_cmp,
                     w_out, g, b):
    bsz, d = x.shape
    n_pages = page_table.shape[1]
    past = n_pages * PAGE_SIZE
    w2 = 2 * KV_WIDTH
    n_in = w_in.shape[1]
    n_pad = -(-n_in // LANES) * LANES
    w_pad = jnp.concatenate([w_in, jnp.zeros((d, n_pad - n_in), w_in.dtype)], axis=1).astype(BF16)
    proj = linear(x, w_pad, tm=bsz, tn=n_pad)
    q = proj[:, :Q_WIDTH].reshape(bsz, NSA_HEADS, HEAD_DIM)
    kv_new = proj[:, Q_WIDTH:Q_WIDTH + 3 * w2]
    gate_pre = proj[:, Q_WIDTH + 3 * w2:n_in].reshape(bsz, 3, NSA_HEADS).transpose(0, 2, 1)
    bg = b_gate.reshape(3, NSA_HEADS).T

    kct, vct = nsa_compress_sample(cache_cmp_t, layer, page_table, w_cmp, pe_cmp)
    ncp = kct.shape[2]
    n_cmp = (past - CMP_LEN) // CMP_STRIDE + 1
    assert (past + 1 - CMP_LEN) // CMP_STRIDE + 1 == n_cmp
    n_slc = -(-(past + 1) // SLC_BLOCK)
    nsp = -(-n_slc // LANES) * LANES
    cmap = jnp.asarray(_cmap_t(n_cmp, n_slc, nsp, ncp).T, BF16)
    heads = np.arange(NSA_HEADS)
    g16 = jnp.asarray(heads[:, None] // NSA_GROUP == heads[None, :] // NSA_GROUP, BF16)
    expand = jnp.asarray(np.arange(nsp)[:, None] == np.arange(past)[None, :] // SLC_BLOCK, BF16)

    wb = cache_win_t.shape[-1]
    const = lambda a: pl.BlockSpec(a.shape, lambda b, pt: (0,) * a.ndim)
    n_seq = SAMPLE_SEQS_PER_STEP
    per_b = lambda *shape: pl.BlockSpec((n_seq,) + shape, lambda b, pt: (b,) + (0,) * len(shape))
    page_specs = _page_specs(cache_slc_t, layer, n_seq, n_pages)
    win_spec = pl.BlockSpec((1, n_seq) + cache_win_t.shape[2:], lambda b, pt: (layer, b, 0, 0, 0, 0))
    o = pl.pallas_call(
        functools.partial(_nsa_sample_attn_kernel, past=past, n_cmp=n_cmp, n_slc=n_slc),
        grid_spec=pltpu.PrefetchScalarGridSpec(
            num_scalar_prefetch=1, grid=(bsz // n_seq,),
            in_specs=[per_b(NSA_HEADS, HEAD_DIM), per_b(1, 3 * w2), per_b(NSA_HEADS, 3), const(bg),
                      per_b(KV_WIDTH, ncp), per_b(KV_WIDTH, ncp), win_spec,
                      const(cmap), const(g16), const(expand)] + page_specs,
            out_specs=per_b(NSA_HEADS, HEAD_DIM)),
        out_shape=jax.ShapeDtypeStruct((bsz, NSA_HEADS, HEAD_DIM), F32),
        compiler_params=_params("parallel"),
        name="nsa_sample_attention",
    )(page_table, q, kv_new.reshape(bsz, 1, 3 * w2), gate_pre, bg, kct, vct, cache_win_t, cmap, g16, expand,
      *([cache_slc_t] * (n_seq * n_pages)))
    y = proj_deepnorm(o.reshape(bsz, Q_WIDTH), w_out.astype(BF16), x, g, b, tm=bsz)
    return y, kv_new


HGRN_C = 128
HGRN_SUB = 8
HGRN_LEVELS = (8, 16, 32, 64)


def _hgrn_lower_bound(logits, layer):
    e = jnp.exp(logits - jnp.max(logits, axis=0, keepdims=True))
    sm = e / jnp.sum(e, axis=0, keepdims=True)
    lb = jnp.zeros_like(sm[0])
    for j in range(1, layer + 1):
        lb = lb + sm[j]
    return lb


def _hgrn_gates(f_raw, lb):
    sg = jax.nn.sigmoid(f_raw)
    f = lb + (1.0 - lb) * sg
    log_f = jnp.where(f > 0.0, jnp.log(f), jnp.log1p(-lb) + f_raw)
    return log_f, (1.0 - lb) * (1.0 - sg)


def _rms_gate(o, gain, g_raw):
    o = o * lax.rsqrt(jnp.mean(o * o, axis=-1, keepdims=True) + LN_EPS) * gain
    return o * (g_raw * jax.nn.sigmoid(g_raw))


def _hgrn_prompt_kernel(x_ref, wq_ref, wf_ref, wi_ref, wg_ref, lb_ref, gain_ref, tri_ref, lmask_ref,
                        y_ref, s_ref, st_ref, cum_ref, *, layer):
    c = HGRN_C
    nblk = c // HGRN_SUB
    lb_all = _hgrn_lower_bound(lb_ref[:, 0], layer)
    st_ref[...] = jnp.zeros_like(st_ref)
    rsub = lax.broadcasted_iota(jnp.int32, (c, 1), 0) % HGRN_SUB

    def roll8(a, j):
        return pltpu.roll(a.reshape(nblk, HGRN_SUB, a.shape[1]), j, 1).reshape(a.shape)

    def head_chunk(hi, rows, q, f_raw, v, g_raw):
        lf, k = _hgrn_gates(f_raw, lb_all[:, hi * HGRN_DK:(hi + 1) * HGRN_DK])
        cum = sum(_dot(tri_ref[...], piece) for piece in _split3(lf))
        cum_ref[hi] = cum
        ends = cum_ref[hi, pl.ds(HGRN_SUB - 1, nblk, stride=HGRN_SUB), :]
        end_b = [jnp.broadcast_to(ends[i:i + 1, :], (HGRN_SUB, HGRN_DK)) for i in range(nblk)]
        zero_b = jnp.zeros((HGRN_SUB, HGRN_DK), F32)
        vb = v.astype(BF16)

        a_off = jnp.zeros((c, c), F32)
        for li, lev in enumerate(HGRN_LEVELS):
            per = lev // HGRN_SUB
            before = [end_b[(i // per) * per - 1] if i >= per else zero_b for i in range(nblk)]
            after = [end_b[(i // per) * per + per - 1] for i in range(nblk)]
            qd = q * jnp.exp(cum - jnp.concatenate(before, axis=0))
            kd = k * jnp.exp(jnp.concatenate(after, axis=0) - cum)
            a_off = a_off + _dot_nt(qd.astype(BF16), kd.astype(BF16)) * lmask_ref[li]
        o = _dot(a_off.astype(BF16), vb)

        for j in range(HGRN_SUB):
            if j == 0:
                w = q * k
                vj = v
            else:
                ok = rsub >= j
                w = jnp.where(ok, q * roll8(k, j) * jnp.exp(cum - roll8(cum, j)), 0.0)
                vj = roll8(v, j)
            o = o + jnp.sum(w, axis=1, keepdims=True) * vj

        last = end_b[nblk - 1][0:1, :]
        o = o + _dot_nt((q * jnp.exp(cum)).astype(BF16), st_ref[hi].astype(BF16))
        kd = (k * jnp.exp(last - cum)).astype(BF16)
        st_ref[hi] = st_ref[hi] * jnp.exp(last) + _dot(v.T.astype(BF16), kd)
        y_ref[0, rows, hi * HGRN_DV:(hi + 1) * HGRN_DV] = _rms_gate(o, gain_ref[...], g_raw)

    n_heads = st_ref.shape[0]

    def chunk(ci, _):
        rows = pl.ds(pl.multiple_of(ci * c, c), c)
        xb = x_ref[0, rows, :].astype(BF16)
        q, f_raw, v, g_raw = (_dot(xb, w[...]) for w in (wq_ref, wf_ref, wi_ref, wg_ref))
        for hi in range(n_heads):
            cols = slice(hi * HGRN_DK, (hi + 1) * HGRN_DK)
            head_chunk(hi, rows, q[:, cols], f_raw[:, cols], v[:, cols], g_raw[:, cols])
        return 0

    lax.fori_loop(0, x_ref.shape[1] // c, chunk, 0)
    for hi in range(n_heads):
        s_ref[0, hi] = st_ref[hi].T


HGRN_HEADS_PER_STEP = 2


def hgrn_prompt(x, w_in_bf, lb_logits, gain, layer):
    bsz, t, d = x.shape
    c = HGRN_C
    hh = HGRN_HEADS
    hps = HGRN_HEADS_PER_STEP
    wide = hps * HGRN_DK
    nl = lb_logits.shape[0]
    tri = jnp.asarray(np.tril(np.ones((c, c), np.float32)), BF16)
    ti = np.arange(c)[:, None]
    si = np.arange(c)[None, :]
    lmask = jnp.asarray(np.stack([((ti // (2 * lev) == si // (2 * lev)) & ((ti // lev) % 2 == 1)
                                   & ((si // lev) % 2 == 0)) for lev in HGRN_LEVELS]), F32)
    w_cols = lambda part: pl.BlockSpec((d, wide), lambda b, h: (0, part * (hh // hps) + h))
    return pl.pallas_call(
        functools.partial(_hgrn_prompt_kernel, layer=layer),
        grid=(bsz, hh // hps),
        in_specs=[pl.BlockSpec((1, t, d), lambda b, h: (b, 0, 0)),
                  w_cols(0), w_cols(1), w_cols(2), w_cols(3),
                  pl.BlockSpec((nl, 1, 1, wide), lambda b, h: (0, h, 0, 0)),
                  pl.BlockSpec((1, HGRN_DV), lambda b, h: (0, 0)),
                  pl.BlockSpec((c, c), lambda b, h: (0, 0)),
                  pl.BlockSpec((len(HGRN_LEVELS), c, c), lambda b, h: (0, 0, 0))],
        out_specs=(pl.BlockSpec((1, t, hps * HGRN_DV), lambda b, h: (b, 0, h)),
                   pl.BlockSpec((1, hps, HGRN_DK, HGRN_DV), lambda b, h: (b, h, 0, 0))),
        out_shape=(jax.ShapeDtypeStruct((bsz, t, hh * HGRN_DV), F32),
                   jax.ShapeDtypeStruct((bsz, hh, HGRN_DK, HGRN_DV), F32)),
        scratch_shapes=[pltpu.VMEM((hps, HGRN_DV, HGRN_DK), F32), pltpu.VMEM((hps, c, HGRN_DK), F32)],
        compiler_params=_params("parallel", "arbitrary"),
        name="hgrn_prompt",
    )(x, w_in_bf, w_in_bf, w_in_bf, w_in_bf, lb_logits.reshape(nl, hh // hps, 1, wide),
      gain.reshape(1, HGRN_DV), tri, lmask)


def _hgrn_sample_kernel(p_ref, s0_ref, lb_ref, gain_ref, y_ref, s_ref, *, layer):
    hh = HGRN_HEADS
    n_seq = p_ref.shape[0]
    lb = _hgrn_lower_bound(lb_ref[...], layer)
    parts, rows = [], []
    for si in range(n_seq):
        p = p_ref[si]
        q8, f8, i8, g8 = (p[j * hh:(j + 1) * hh, :] for j in range(4))
        lf8, k8 = _hgrn_gates(f8, lb)
        parts.append((q8, k8, i8, g8))
        rows.append(jnp.exp(lf8))
    pad = [jnp.zeros((LANES - hh * n_seq, HGRN_DK), F32)] if hh * n_seq < LANES else []
    cols = jnp.concatenate(rows + pad, axis=0).T
    zeros = jnp.zeros((BF16_ROWS - 1, HGRN_DK), BF16)
    head_row = lax.broadcasted_iota(jnp.int32, (hh, 1), 0)
    for si in range(n_seq):
        q8, k8, i8, g8 = parts[si]
        q8b = q8.astype(BF16)
        for h in range(hh):
            k16 = jnp.concatenate([k8[h:h + 1, :].astype(BF16), zeros], axis=0)
            i16 = jnp.concatenate([i8[h:h + 1, :].astype(BF16), zeros], axis=0)
            outer = lax.dot_general(k16, i16, (((0,), (0,)), ((), ())), preferred_element_type=F32)
            s_new = cols[:, hh * si + h:hh * si + h + 1] * s0_ref[si, h] + outer
            s_ref[si, h] = s_new
            o_h = _dot(q8b, s_new.astype(BF16))
            o = o_h if h == 0 else jnp.where(head_row == h, o_h, o)
        y_ref[si] = _rms_gate(o, gain_ref[...], g8)


HGRN_SAMPLE_SEQS = 8


def hgrn_sample(proj, s0_all, j, lb_logits, gain, layer):
    bsz = proj.shape[0]
    hh = HGRN_HEADS
    nl = lb_logits.shape[0]
    n_seq = HGRN_SAMPLE_SEQS
    assert bsz % n_seq == 0
    y, s_new = pl.pallas_call(
        functools.partial(_hgrn_sample_kernel, layer=layer),
        grid=(bsz // n_seq,),
        in_specs=[pl.BlockSpec((n_seq, 4 * hh, HGRN_DK), lambda b: (b, 0, 0)),
                  pl.BlockSpec((None, n_seq, hh, HGRN_DK, HGRN_DV), lambda b: (j, b, 0, 0, 0)),
                  pl.BlockSpec((nl, hh, HGRN_DK), lambda b: (0, 0, 0)),
                  pl.BlockSpec((1, HGRN_DV), lambda b: (0, 0))],
        out_specs=(pl.BlockSpec((n_seq, hh, HGRN_DV), lambda b: (b, 0, 0)),
                   pl.BlockSpec((n_seq, hh, HGRN_DK, HGRN_DV), lambda b: (b, 0, 0, 0))),
        out_shape=(jax.ShapeDtypeStruct((bsz, hh, HGRN_DV), F32),
                   jax.ShapeDtypeStruct(s0_all.shape[1:], F32)),
        compiler_params=_params("parallel"),
        name="hgrn_sample",
    )(proj.reshape(bsz, 4 * hh, HGRN_DK), s0_all, lb_logits.reshape(nl, hh, HGRN_DK), gain.reshape(1, HGRN_DV))
    return y.reshape(bsz, hh * HGRN_DV), s_new


TM_PROJ = 512
TM_FFN = 1024
FFN_HC = 256
TM_GMLP = 256
TM_LINEAR = 512
TN_LINEAR = 1024


def kernel(x_prompt, x_sample, cache_cmp_kv, cache_slc_kv, cache_win_kv, state_hgrn, page_table, ln_gain, ln_bias, ffn_w_in, ffn_w_out, nsa_w_in, nsa_b_gate, nsa_w_cmp, nsa_pe_cmp, nsa_w_out, gmlp_w_in, gmlp_b_in, gmlp_ln_v, gmlp_w_sp, gmlp_b_sp, gmlp_w_out, hgrn_w_in, hgrn_lb_logits, hgrn_norm_gain, hgrn_w_out):
    bsz, t, d = x_prompt.shape
    sb = x_sample.shape[0]
    assert x_sample.shape[1] == 1
    xp = x_prompt
    xs = x_sample.reshape(sb, d)
    kv_shape = (2, NSA_KV_HEADS, HEAD_DIM)
    wb_prompt = min(WINDOW, t)
    cmp_t, slc_t, win_t = _token_minor(cache_cmp_kv), _token_minor(cache_slc_kv), _token_minor(cache_win_kv)
    cmp_s, slc_s, win_s, gv_s, hs_p, hs_s = [], [], [], [], [], []
    n_nsa = (DEPTH + N_MIXERS - 1) // N_MIXERS
    kv_pages = None
    ffn_in_bf, ffn_out_bf = ffn_w_in.astype(BF16), ffn_w_out.astype(BF16)
    for layer in range(DEPTH):
        kind, j = layer % N_MIXERS, layer // N_MIXERS
        g0, b0 = ln_gain[layer, 0], ln_bias[layer, 0]
        if kind == 0:
            xp, kv_pages = nsa_prompt_layer(xp, nsa_w_in[j], nsa_b_gate[j], nsa_w_cmp[j], nsa_pe_cmp[j],
                                            nsa_w_out[j], g0, b0, j, n_nsa, kv_pages)
            xs, kv_new = nsa_sample_layer(xs, cmp_t, slc_t, win_t, j, page_table,
                                          nsa_w_in[j], nsa_b_gate[j], nsa_w_cmp[j], nsa_pe_cmp[j], nsa_w_out[j],
                                          g0, b0)
            w2 = 2 * KV_WIDTH
            cmp_s.append(kv_new[:, 0:w2].reshape((sb, 1) + kv_shape))
            slc_s.append(kv_new[:, w2:2 * w2].reshape((sb, 1) + kv_shape))
            win_s.append(kv_new[:, 2 * w2:3 * w2].reshape((sb, 1) + kv_shape))
        elif kind == 1:
            w_in_bf, w_out_bf = gmlp_w_in[j].astype(BF16), gmlp_w_out[j].astype(BF16)
            xp = gmlp_layer(xp.reshape(bsz * t, d), w_in_bf, gmlp_b_in[j], gmlp_ln_v[j], gmlp_w_sp[j], gmlp_b_sp[j],
                            w_out_bf, g0, b0, chunked=True, tm=TM_GMLP).reshape(bsz, t, d)
            xs, v_new = gmlp_layer(xs, w_in_bf, gmlp_b_in[j], gmlp_ln_v[j], gmlp_w_sp[j], gmlp_b_sp[j],
                                   w_out_bf, g0, b0, chunked=False, tm=sb)
            gv_s.append(v_new.reshape(sb, 1, GMLP_WIDTH))
        else:
            w_in_bf, w_out_bf = hgrn_w_in[j].astype(BF16), hgrn_w_out[j].astype(BF16)
            y, s_p = hgrn_prompt(xp, w_in_bf, hgrn_lb_logits, hgrn_norm_gain[j], layer)
            xp = proj_deepnorm(y.reshape(bsz * t, HGRN_WIDTH), w_out_bf, xp.reshape(bsz * t, d), g0, b0,
                               TM_PROJ).reshape(bsz, t, d)
            proj_s = linear(xs, w_in_bf, sb, TN_LINEAR)
            y_s, s_s = hgrn_sample(proj_s, state_hgrn, j, hgrn_lb_logits, hgrn_norm_gain[j], layer)
            xs = proj_deepnorm(y_s, w_out_bf, xs, g0, b0, sb)
            hs_p.append(s_p)
            hs_s.append(s_s)
        g1, b1 = ln_gain[layer, 1], ln_bias[layer, 1]
        xp = ffn_deepnorm(xp.reshape(bsz * t, d), ffn_in_bf, ffn_out_bf, layer, g1, b1, TM_FFN, FFN_HC).reshape(bsz, t, d)
        xs = ffn_deepnorm(xs, ffn_in_bf, ffn_out_bf, layer, g1, b1, sb, FFN_HC)
    kvc, kvs, kvw = kv_pages
    n_page = t // PAGE_SIZE
    pages = lambda a: _token_major(a.reshape((n_nsa, bsz * n_page) + a.shape[3:]))
    win_rows = _token_major(kvw[:, :, n_page - wb_prompt // PAGE_SIZE:])
    win_p = win_rows.reshape((n_nsa, bsz, wb_prompt) + kv_shape)
    return (xp, xs.reshape(sb, 1, d), pages(kvc), jnp.stack(cmp_s), pages(kvs), jnp.stack(slc_s),
            win_p, jnp.stack(win_s), jnp.stack(gv_s), jnp.stack(hs_p), jnp.stack(hs_s))
```

```python
import functools

import jax
import jax.numpy as jnp
import numpy as np
from jax import lax
from jax.experimental import pallas as pl
from jax.experimental.pallas import tpu as pltpu

F32 = jnp.float32
BF16 = jnp.bfloat16

D_MODEL = 1024
DEPTH = 4
PAGE_SIZE = 128
N_MIXERS = 3

NSA_HEADS = 16
NSA_KV_HEADS = 4
NSA_GROUP = NSA_HEADS // NSA_KV_HEADS
HEAD_DIM = D_MODEL // NSA_HEADS
Q_WIDTH = NSA_HEADS * HEAD_DIM
KV_WIDTH = NSA_KV_HEADS * HEAD_DIM
CMP_LEN = 32
CMP_STRIDE = 16
SLC_BLOCK = 64
TOP_N = 16
WINDOW = 512
FORCE_BONUS = 1e3
ATTN_SCALE = HEAD_DIM ** -0.5
NEG_INF = -1e30

GMLP_WIDTH = D_MODEL
GMLP_GROUPS = 8
GMLP_GROUP_DIM = GMLP_WIDTH // GMLP_GROUPS
GMLP_CHUNK = 128

HGRN_HEADS = 8
HGRN_DK = 128
HGRN_DV = 128
HGRN_WIDTH = HGRN_HEADS * HGRN_DK

FFN_HIDDEN = 2816
DEEPNORM_ALPHA = (2 * DEPTH) ** 0.25
LN_EPS = 1e-5

LANES = 128
BF16_ROWS = 16
VMEM_LIMIT = 48 * 1024 * 1024


def _params(*sem):
    return pltpu.CompilerParams(dimension_semantics=sem, vmem_limit_bytes=VMEM_LIMIT)


def _dot(a, b):
    return jnp.dot(a, b, preferred_element_type=F32)


def _dot_nt(a, b):
    return lax.dot_general(a, b, (((1,), (1,)), ((), ())), preferred_element_type=F32)


def _layer_norm_rows(y, g, b):
    mu = jnp.mean(y, axis=-1, keepdims=True)
    yc = y - mu
    var = jnp.mean(yc * yc, axis=-1, keepdims=True)
    return yc * lax.rsqrt(var + LN_EPS) * g + b


def _deepnorm_rows(x, h, g, b):
    return _layer_norm_rows(DEEPNORM_ALPHA * x + h, g, b)


def _split3(x):
    hi = x.astype(BF16)
    r1 = x - hi.astype(F32)
    mid = r1.astype(BF16)
    lo = (r1 - mid.astype(F32)).astype(BF16)
    return hi, mid, lo


def _proj_deepnorm_kernel(a_ref, w_ref, x_ref, g_ref, b_ref, o_ref):
    h = _dot(a_ref[...].astype(BF16), w_ref[...])
    o_ref[...] = _deepnorm_rows(x_ref[...], h, g_ref[...], b_ref[...])


def proj_deepnorm(a, w_bf, x, g, b, tm):
    m, k = a.shape
    d = x.shape[1]
    tm = min(tm, m)
    return pl.pallas_call(
        _proj_deepnorm_kernel,
        grid=(m // tm,),
        in_specs=[pl.BlockSpec((tm, k), lambda i: (i, 0)),
                  pl.BlockSpec((k, d), lambda i: (0, 0)),
                  pl.BlockSpec((tm, d), lambda i: (i, 0)),
                  pl.BlockSpec((1, d), lambda i: (0, 0)),
                  pl.BlockSpec((1, d), lambda i: (0, 0))],
        out_specs=pl.BlockSpec((tm, d), lambda i: (i, 0)),
        out_shape=jax.ShapeDtypeStruct((m, d), F32),
        compiler_params=_params("parallel"),
        name="proj_deepnorm",
    )(a, w_bf, x, g.reshape(1, d), b.reshape(1, d))


def _ffn_kernel(x_ref, wg_ref, wu_ref, wo_ref, g_ref, b_ref, o_ref, xb_ref, acc_ref):
    j = pl.program_id(1)

    @pl.when(j == 0)
    def _():
        xb_ref[...] = x_ref[...].astype(BF16)
        acc_ref[...] = jnp.zeros_like(acc_ref)

    xb = xb_ref[...]
    gate = _dot(xb, wg_ref[...])
    up = _dot(xb, wu_ref[...])
    mid = (gate * jax.nn.sigmoid(gate) * up).astype(BF16)
    acc_ref[...] += _dot(mid, wo_ref[...])

    @pl.when(j == pl.num_programs(1) - 1)
    def _():
        o_ref[...] = _deepnorm_rows(x_ref[...], acc_ref[...], g_ref[...], b_ref[...])


def ffn_deepnorm(x, w_in_bf, w_out_bf, layer, g, b, tm, hc):
    m, d = x.shape
    hidden = w_out_bf.shape[1]
    tm = min(tm, m)
    nh = hidden // hc
    return pl.pallas_call(
        _ffn_kernel,
        grid=(m // tm, nh),
        in_specs=[pl.BlockSpec((tm, d), lambda i, j: (i, 0)),
                  pl.BlockSpec((None, d, hc), lambda i, j: (layer, 0, j)),
                  pl.BlockSpec((None, d, hc), lambda i, j: (layer, 0, nh + j)),
                  pl.BlockSpec((None, hc, d), lambda i, j: (layer, j, 0)),
                  pl.BlockSpec((1, d), lambda i, j: (0, 0)),
                  pl.BlockSpec((1, d), lambda i, j: (0, 0))],
        out_specs=pl.BlockSpec((tm, d), lambda i, j: (i, 0)),
        out_shape=jax.ShapeDtypeStruct((m, d), F32),
        scratch_shapes=[pltpu.VMEM((tm, d), BF16), pltpu.VMEM((tm, d), F32)],
        compiler_params=_params("parallel", "arbitrary"),
        name="ffn_deepnorm",
    )(x, w_in_bf, w_in_bf, w_out_bf, g.reshape(1, d), b.reshape(1, d))


def _gelu_exact(z):
    return 0.5 * z * (1.0 + lax.erf(z * (0.5 ** 0.5)))


def _gmlp_kernel(x_ref, win_ref, bin_ref, lnv_ref, wsp_ref, bsp_ref, wout_ref, g_ref, b_ref,
                 o_ref, *rest, chunked):
    width = GMLP_WIDTH
    x = x_ref[...]
    z = _dot(x.astype(BF16), win_ref[...]) + bin_ref[...]
    gz = _gelu_exact(z)
    u = gz[:, :width]
    v = _layer_norm_rows(gz[:, width:], lnv_ref[0:1, :], lnv_ref[1:2, :])
    if chunked:
        mix_ref, = rest
        vb = v.astype(BF16)
        for c in range(x.shape[0] // GMLP_CHUNK):
            rows = slice(c * GMLP_CHUNK, (c + 1) * GMLP_CHUNK)
            for h in range(GMLP_GROUPS):
                cols = slice(h * GMLP_GROUP_DIM, (h + 1) * GMLP_GROUP_DIM)
                mix_ref[rows, cols] = _dot(wsp_ref[h], vb[rows, cols]) + bsp_ref[:, cols]
        mixed = mix_ref[...]
    else:
        v_ref, = rest
        v_ref[...] = v
        mixed = v * wsp_ref[...] + bsp_ref[...]
    h_out = _dot((u * mixed).astype(BF16), wout_ref[...])
    o_ref[...] = _deepnorm_rows(x, h_out, g_ref[...], b_ref[...])


def gmlp_layer(x, w_in_bf, b_in, ln_v, w_sp, b_sp, w_out_bf, g, b, *, chunked, tm):
    m, d = x.shape
    width = GMLP_WIDTH
    tm = min(tm, m)
    const = lambda *shape: pl.BlockSpec(shape, lambda i: (0,) * len(shape))
    if chunked:
        wsp = jnp.tril(w_sp).astype(BF16)
        bsp = jnp.repeat(b_sp.T, GMLP_GROUP_DIM, axis=1)
        wsp_spec, bsp_spec = const(GMLP_GROUPS, GMLP_CHUNK, GMLP_CHUNK), const(GMLP_CHUNK, width)
        out_shape = jax.ShapeDtypeStruct((m, d), F32)
        out_specs = pl.BlockSpec((tm, d), lambda i: (i, 0))
        scratch = [pltpu.VMEM((tm, width), F32)]
    else:
        wsp = jnp.repeat(w_sp[:, 0, 0], GMLP_GROUP_DIM).reshape(1, width)
        bsp = jnp.repeat(b_sp[:, 0], GMLP_GROUP_DIM).reshape(1, width)
        wsp_spec, bsp_spec = const(1, width), const(1, width)
        out_shape = (jax.ShapeDtypeStruct((m, d), F32), jax.ShapeDtypeStruct((m, width), F32))
        out_specs = (pl.BlockSpec((tm, d), lambda i: (i, 0)), pl.BlockSpec((tm, width), lambda i: (i, 0)))
        scratch = []
    return pl.pallas_call(
        functools.partial(_gmlp_kernel, chunked=chunked),
        grid=(m // tm,),
        in_specs=[pl.BlockSpec((tm, d), lambda i: (i, 0)),
                  const(d, 2 * width), const(1, 2 * width), const(2, width),
                  wsp_spec, bsp_spec, const(width, d), const(1, d), const(1, d)],
        out_specs=out_specs,
        out_shape=out_shape,
        scratch_shapes=scratch,
        compiler_params=_params("parallel"),
        name="gmlp_layer",
    )(x, w_in_bf, b_in.reshape(1, 2 * width), ln_v, wsp, bsp, w_out_bf, g.reshape(1, d), b.reshape(1, d))


def _linear_kernel(x_ref, w_ref, o_ref):
    o_ref[...] = _dot(x_ref[...].astype(BF16), w_ref[...])


def linear(x, w_bf, tm, tn):
    m, k = x.shape
    n = w_bf.shape[1]
    tm, tn = min(tm, m), min(tn, n)
    return pl.pallas_call(
        _linear_kernel,
        grid=(m // tm, n // tn),
        in_specs=[pl.BlockSpec((tm, k), lambda i, j: (i, 0)),
                  pl.BlockSpec((k, tn), lambda i, j: (0, j))],
        out_specs=pl.BlockSpec((tm, tn), lambda i, j: (i, j)),
        out_shape=jax.ShapeDtypeStruct((m, n), F32),
        compiler_params=_params("parallel", "parallel"),
        name="linear",
    )(x, w_bf)


def _token_minor(cache):
    nd = cache.ndim
    return jnp.transpose(cache, tuple(range(nd - 4)) + (nd - 3, nd - 2, nd - 1, nd - 4))


def _token_major(kvt):
    nd = kvt.ndim
    return jnp.transpose(kvt, tuple(range(nd - 4)) + (nd - 1, nd - 4, nd - 3, nd - 2))


CMP_SLABS = 2 * KV_WIDTH // LANES


def _cmp_weights(w_cmp):
    w = w_cmp.reshape(2, 2, CMP_STRIDE, HEAD_DIM, HEAD_DIM)
    eye = jnp.eye(NSA_KV_HEADS, dtype=w_cmp.dtype)
    wbd = jnp.einsum('gh,klpde->pkgdlhe', eye, w)
    return wbd.reshape(CMP_STRIDE, 2, KV_WIDTH, 2 * KV_WIDTH).astype(BF16)


def _cmp_operands(w_cmp, pe_cmp, k_natural):
    wbd = _cmp_weights(w_cmp)
    wk = wbd[:, 0] if k_natural else jnp.swapaxes(wbd[:, 0], 1, 2)
    wv = jnp.swapaxes(wbd[:, 1], 1, 2)
    pe = pe_cmp.reshape(2, CMP_LEN * HEAD_DIM)
    wflat = w_cmp.reshape(2, CMP_LEN * HEAD_DIM, HEAD_DIM).astype(BF16)
    wflat_t = jnp.swapaxes(wflat, 1, 2)
    const = lambda a: pl.BlockSpec(a.shape, lambda *_: (0,) * a.ndim)
    ops = (wk, wv, pe, wflat, wflat_t)
    return ops, [const(a) for a in ops]


def _compress(load_rows, nhb, period, n_cmp, wk_ref, wv_ref, pe_ref, wflat_ref, wflat_t_ref, k_natural):
    kw = KV_WIDTH
    acc_k = jnp.zeros((nhb, 2 * kw) if k_natural else (2 * kw, nhb), F32)
    acc_v = jnp.zeros((2 * kw, nhb), F32)
    for p in range(CMP_STRIDE):
        slabs = [load_rows(p, s).astype(BF16) for s in range(CMP_SLABS)]
        xk = jnp.concatenate(slabs[:CMP_SLABS // 2], axis=1)
        xv = jnp.concatenate(slabs[CMP_SLABS // 2:], axis=1)
        acc_k = acc_k + (_dot(xk, wk_ref[p]) if k_natural else _dot_nt(wk_ref[p], xk))
        acc_v = acc_v + _dot_nt(wv_ref[p], xv)

    def pe8(kv):
        return jnp.broadcast_to(pe_ref[kv:kv + 1, :], (8, CMP_LEN * HEAD_DIM)).astype(BF16)

    def finish_t(acc, kv):
        bias = _dot_nt(wflat_t_ref[kv], pe8(kv))[:, 0:1]
        bias = jnp.concatenate([bias] * NSA_KV_HEADS, axis=0)
        out = acc[:kw, :] + pltpu.roll(acc[kw:, :], nhb - 1, 1) + bias
        return jnp.where(lax.broadcasted_iota(jnp.int32, (1, nhb), 1) % period < n_cmp, out, 0.0)

    if k_natural:
        bias = _dot(pe8(0), wflat_ref[0])[0:1, :]
        bias = jnp.concatenate([bias] * NSA_KV_HEADS, axis=1)
        k_out = acc_k[:, :kw] + pltpu.roll(acc_k[:, kw:], nhb - 1, 0) + bias
        k_out = jnp.where(lax.broadcasted_iota(jnp.int32, (nhb, 1), 0) % period < n_cmp, k_out, 0.0)
    else:
        k_out = finish_t(acc_k, 0)
    return k_out, finish_t(acc_v, 1)


def _cmap_t(n_cmp, n_slc, rows, cols):
    c0 = np.arange(cols)[None, :] * CMP_STRIDE
    s0 = np.arange(rows)[:, None] * SLC_BLOCK
    ov = np.minimum(c0 + CMP_LEN, s0 + SLC_BLOCK) - np.maximum(c0, s0)
    ov = np.clip(ov, 0, None).astype(np.float32) / CMP_STRIDE
    keep = (np.arange(cols)[None, :] < n_cmp) & (np.arange(rows)[:, None] < n_slc)
    return np.where(keep, ov, 0.0).astype(np.float32)


def _rank_before(score, n_slc, axis, blk):
    rank = jnp.zeros(score.shape, jnp.int32)
    for j in range(n_slc):
        other = score[j:j + 1, :] if axis == 0 else score[:, j:j + 1]
        before = (other > score) | ((other == score) & (j < blk))
        rank = rank + before.astype(jnp.int32)
    return rank


NSA_QB = 128
NSA_KT = 2 * NSA_QB
NSA_HEADS_PER_STEP = 2
NSA_GATE_ROWS = 64
NSA_T_ROWS = Q_WIDTH + 6 * KV_WIDTH + NSA_GATE_ROWS
NSA_NAT_COLS = 4 * KV_WIDTH


def _nsa_proj_kernel(x_ref, wn_ref, wt_ref, bg_ref, *rest, n_prev):
    xc_ref, ks_ref, kw_ref, qt_ref, vst_ref, vwt_ref, gt_ref, kvc_ref, kvs_ref, kvw_ref = rest[n_prev:]
    tq = x_ref.shape[1]
    kw_, hd = KV_WIDTH, HEAD_DIM
    xb = x_ref[0].astype(BF16)
    nat = _dot(xb, wn_ref[...])
    for s in range(CMP_SLABS):
        xc_ref[0, s] = nat[:, s * LANES:(s + 1) * LANES]
    tok = pl.program_id(1) * tq + lax.broadcasted_iota(jnp.int32, (tq, hd), 0)
    onehot = (tok // SLC_BLOCK == lax.broadcasted_iota(jnp.int32, (tq, hd), 1)).astype(BF16)
    for g in range(NSA_KV_HEADS):
        k_slc = nat[:, 2 * kw_ + g * hd:2 * kw_ + (g + 1) * hd].astype(BF16)
        ks_ref[0, g] = jnp.concatenate([k_slc, onehot], axis=1)
        kw_ref[0, g] = nat[:, 3 * kw_ + g * hd:3 * kw_ + (g + 1) * hd].astype(BF16)
    t = _dot_nt(wt_ref[...], xb)
    kv0 = Q_WIDTH
    for g in range(NSA_KV_HEADS):
        r_slc = kv0 + 3 * kw_ + g * hd
        r_win = kv0 + 5 * kw_ + g * hd
        vst_ref[0, g] = t[r_slc:r_slc + hd, :].astype(BF16)
        vwt_ref[0, g] = t[r_win:r_win + hd, :].astype(BF16)
    gates = jax.nn.sigmoid(t[kv0 + 6 * kw_:, :] + bg_ref[...])
    for nb in range(tq // NSA_QB):
        cols = slice(nb * NSA_QB, (nb + 1) * NSA_QB)
        gt_ref[0, nb] = gates[:, cols]
        for br, ref in enumerate((kvc_ref, kvs_ref, kvw_ref)):
            rows = t[kv0 + br * 2 * kw_:kv0 + (br + 1) * 2 * kw_, cols]
            ref[0, nb] = rows.reshape(2, NSA_KV_HEADS, hd, NSA_QB)
        for g in range(NSA_KV_HEADS):
            for r in range(NSA_GROUP):
                r0 = (g * NSA_GROUP + r) * hd
                qt_ref[0, nb, g, :, r * NSA_QB:(r + 1) * NSA_QB] = (
                    t[r0:r0 + hd, cols] * ATTN_SCALE).astype(BF16)


def nsa_prompt_proj(x, w_in, b_gate, tq, layer, n_layers, prev_pages):
    bsz, t, d = x.shape
    assert NSA_QB == PAGE_SIZE and t // SLC_BLOCK <= HEAD_DIM
    nb, nbk = t // NSA_QB, tq // NSA_QB
    g_, hd, w2 = NSA_KV_HEADS, HEAD_DIM, 2 * KV_WIDTH
    w_q = w_in[:, :Q_WIDTH]
    w_kv = w_in[:, Q_WIDTH:Q_WIDTH + 3 * w2]
    w_g = w_in[:, Q_WIDTH + 3 * w2:]
    pad = jnp.zeros((d, NSA_GATE_ROWS - 3 * NSA_HEADS), w_in.dtype)
    w_t = jnp.concatenate([w_q, w_kv, w_g, pad], axis=1).T.astype(BF16)
    w_nat = jnp.concatenate([w_kv[:, :w2], w_kv[:, w2:w2 + KV_WIDTH], w_kv[:, 2 * w2:2 * w2 + KV_WIDTH]],
                            axis=1).astype(BF16)
    bg = jnp.concatenate([b_gate, jnp.zeros((NSA_GATE_ROWS - 3 * NSA_HEADS,), b_gate.dtype)]).reshape(NSA_GATE_ROWS, 1)
    page = jax.ShapeDtypeStruct((n_layers, bsz, nb, 2, g_, hd, NSA_QB), F32)
    page_spec = pl.BlockSpec((None, 1, nbk, 2, g_, hd, NSA_QB), lambda b, i: (layer, b, i, 0, 0, 0, 0))
    out_shape = (
        jax.ShapeDtypeStruct((bsz, CMP_SLABS, t, LANES), F32),
        jax.ShapeDtypeStruct((bsz, g_, t, 2 * hd), BF16), jax.ShapeDtypeStruct((bsz, g_, t, hd), BF16),
        jax.ShapeDtypeStruct((bsz, nb, g_, hd, NSA_GROUP * NSA_QB), BF16),
        jax.ShapeDtypeStruct((bsz, g_, hd, t), BF16), jax.ShapeDtypeStruct((bsz, g_, hd, t), BF16),
        jax.ShapeDtypeStruct((bsz, nb, NSA_GATE_ROWS, NSA_QB), F32),
        page, page, page,
    )
    out_specs = (
        pl.BlockSpec((1, CMP_SLABS, tq, LANES), lambda b, i: (b, 0, i, 0)),
        pl.BlockSpec((1, g_, tq, 2 * hd), lambda b, i: (b, 0, i, 0)),
        pl.BlockSpec((1, g_, tq, hd), lambda b, i: (b, 0, i, 0)),
        pl.BlockSpec((1, nbk, g_, hd, NSA_GROUP * NSA_QB), lambda b, i: (b, i, 0, 0, 0)),
        pl.BlockSpec((1, g_, hd, tq), lambda b, i: (b, 0, 0, i)),
        pl.BlockSpec((1, g_, hd, tq), lambda b, i: (b, 0, 0, i)),
        pl.BlockSpec((1, nbk, NSA_GATE_ROWS, NSA_QB), lambda b, i: (b, i, 0, 0)),
        page_spec, page_spec, page_spec,
    )
    prev = tuple(prev_pages) if prev_pages is not None else ()
    n_out = len(out_shape)
    aliases = {4 + i: n_out - len(prev) + i for i in range(len(prev))}
    return pl.pallas_call(
        functools.partial(_nsa_proj_kernel, n_prev=len(prev)),
        grid=(bsz, t // tq),
        in_specs=[pl.BlockSpec((1, tq, d), lambda b, i: (b, i, 0)),
                  pl.BlockSpec((d, NSA_NAT_COLS), lambda b, i: (0, 0)),
                  pl.BlockSpec((NSA_T_ROWS, d), lambda b, i: (0, 0)),
                  pl.BlockSpec((NSA_GATE_ROWS, 1), lambda b, i: (0, 0))]
                 + [pl.BlockSpec(memory_space=pl.ANY)] * len(prev),
        out_specs=out_specs,
        out_shape=out_shape,
        input_output_aliases=aliases,
        compiler_params=_params("parallel", "parallel"),
        name="nsa_prompt_proj",
    )(x, w_nat, w_t, bg, *prev)


def _nsa_compress_prompt_kernel(x_ref, wk_ref, wv_ref, pe_ref, wflat_ref, wflat_t_ref, kc_ref, vct_ref, *, n_cmp):
    nhb = x_ref.shape[2] // CMP_STRIDE
    load = lambda p, s: x_ref[0, s, pl.ds(p, nhb, stride=CMP_STRIDE), :]
    k_all, v_t = _compress(load, nhb, nhb, n_cmp, wk_ref, wv_ref, pe_ref, wflat_ref, wflat_t_ref, True)
    for g in range(NSA_KV_HEADS):
        kc_ref[0, g] = k_all[:, g * HEAD_DIM:(g + 1) * HEAD_DIM].astype(BF16)
        vct_ref[0, g] = v_t[g * HEAD_DIM:(g + 1) * HEAD_DIM, :].astype(BF16)


def nsa_compress_prompt(kv_cmp, w_cmp, pe_cmp):
    bsz, _, t, _ = kv_cmp.shape
    nhb = t // CMP_STRIDE
    n_cmp = (t - CMP_LEN) // CMP_STRIDE + 1
    ops, specs = _cmp_operands(w_cmp, pe_cmp, True)
    return pl.pallas_call(
        functools.partial(_nsa_compress_prompt_kernel, n_cmp=n_cmp),
        grid=(bsz,),
        in_specs=[pl.BlockSpec((1, CMP_SLABS, t, LANES), lambda b: (b, 0, 0, 0))] + specs,
        out_specs=(pl.BlockSpec((1, NSA_KV_HEADS, nhb, HEAD_DIM), lambda b: (b, 0, 0, 0)),
                   pl.BlockSpec((1, NSA_KV_HEADS, HEAD_DIM, nhb), lambda b: (b, 0, 0, 0))),
        out_shape=(jax.ShapeDtypeStruct((bsz, NSA_KV_HEADS, nhb, HEAD_DIM), BF16),
                   jax.ShapeDtypeStruct((bsz, NSA_KV_HEADS, HEAD_DIM, nhb), BF16)),
        compiler_params=_params("parallel"),
        name="nsa_compress_prompt",
    )(kv_cmp, *ops)


def _attn_bias_consts():
    k = np.arange(NSA_QB)[:, None]
    q = np.arange(NSA_QB)[None, :]
    x4 = lambda a: np.tile(a.astype(np.float32), (1, NSA_GROUP))
    causal = x4(np.where(k <= q, 0.0, NEG_INF))
    lower = x4(np.where(k > q, 0.0, NEG_INF))
    zeros, neg = np.zeros_like(causal), np.full_like(causal, NEG_INF)
    win_bias = np.stack([zeros, causal, neg, lower])
    slc_bias = np.stack([np.concatenate([causal, neg]), np.concatenate([zeros, causal])])
    return win_bias, slc_bias


WB_ZERO, WB_CAUSAL, WB_NEG, WB_LOWER = 0, 1, 2, 3


def _softmax_cols(s, mask):
    sm = jnp.where(mask, s, NEG_INF)
    m = jnp.max(sm, axis=0, keepdims=True)
    e = jnp.where(mask, jnp.exp(sm - m), 0.0)
    l = jnp.sum(e, axis=0, keepdims=True)
    return e / jnp.where(l > 0.0, l, 1.0)


def _lanes_x4(a):
    return jnp.concatenate([a] * NSA_GROUP, axis=1)


SOFTMAX_CHUNK = 128


def _softmax_pv(score_tile, n_tiles, tile, s_ref, p_ref, vt):
    n_rows = n_tiles * tile
    width = s_ref.shape[1]
    sub = 8
    by_vreg = lambda a: a.reshape(a.shape[0] // sub, sub, width)
    m8 = None
    for j in range(n_tiles):
        s = score_tile(j)
        s_ref[j * tile:(j + 1) * tile, :] = s
        m_j = jnp.max(by_vreg(s), axis=0)
        m8 = m_j if m8 is None else jnp.maximum(m8, m_j)
    m = jnp.max(m8, axis=0, keepdims=True)
    l8 = jnp.zeros((sub, width), F32)
    for r0 in range(0, n_rows, SOFTMAX_CHUNK):
        rows = slice(r0, r0 + SOFTMAX_CHUNK)
        e = jnp.exp(s_ref[rows, :] - m)
        l8 = l8 + jnp.sum(by_vreg(e), axis=0)
        p_ref[rows, :] = e.astype(BF16)
    l = jnp.sum(l8, axis=0, keepdims=True)
    return _dot(vt, p_ref[0:n_rows, :]) / l


def _nsa_prompt_attn_kernel(qt_ref, gt_ref, kc_ref, vct_ref, ks_ref, vst_ref, kw_ref, vwt_ref, cmap_ref,
                            wbias_ref, sbias_ref, o_ref, s_ref, p_ref, oslc_ref, *, n_cmp, n_slc, nb):
    qb, kt = NSA_QB, NSA_KT
    width = NSA_GROUP * qb
    n_heads = qt_ref.shape[2]
    g0 = pl.program_id(1) * n_heads
    n = pl.program_id(2)
    s0 = n * qb
    qpos1 = s0 + lax.broadcasted_iota(jnp.int32, (1, qb), 1)
    qpos = _lanes_x4(qpos1)
    ncp = kc_ref.shape[2]
    nsp = cmap_ref.shape[0]

    def compressed_and_select(gi):
        qt = qt_ref[0, 0, gi]
        crow = lax.broadcasted_iota(jnp.int32, (ncp, 1), 0)
        cmask = (crow * CMP_STRIDE + (CMP_LEN - 1) <= qpos) & (crow < n_cmp)
        p_cmp = _softmax_cols(_dot(kc_ref[0, gi], qt), cmask)
        o_cmp = _dot(vct_ref[0, gi], p_cmp.astype(BF16))

        psum = p_cmp[:, 0:qb]
        for r in range(1, NSA_GROUP):
            psum = psum + p_cmp[:, r * qb:(r + 1) * qb]
        imp = sum(_dot(cmap_ref[...], piece) for piece in _split3(psum))
        blk = lax.broadcasted_iota(jnp.int32, (nsp, 1), 0)
        cur = qpos1 // SLC_BLOCK
        forced = (blk == 0) | (blk == cur) | (blk == cur - 1)
        allowed = (blk <= cur) & (blk < n_slc)
        score = jnp.where(allowed, imp + jnp.where(forced, FORCE_BONUS, 0.0), -jnp.inf)
        sel = (_rank_before(score, n_slc, 0, blk) < min(TOP_N, n_slc)) & allowed
        sel_bias = _lanes_x4(jnp.where(sel, 0.0, NEG_INF)).astype(BF16)
        q_ext = jnp.concatenate([qt, sel_bias, jnp.zeros((HEAD_DIM - nsp, width), BF16)], axis=0)
        return qt, o_cmp, q_ext

    heads = [compressed_and_select(gi) for gi in range(n_heads)]

    last = n // (kt // qb)
    for k_last in range(nb // (kt // qb)):
        @pl.when(last == k_last)
        def _():
            for gi in range(n_heads):
                q_ext = heads[gi][2]

                def score_tile(j):
                    s = _dot(ks_ref[0, gi, j * kt:(j + 1) * kt, :], q_ext)
                    return s + sbias_ref[n % (kt // qb)] if j == k_last else s

                oslc_ref[gi] = _softmax_pv(score_tile, k_last + 1, kt, s_ref.at[gi], p_ref.at[gi],
                                           vst_ref[0, gi, :, 0:(k_last + 1) * kt])

    n_win = WINDOW // qb
    n_wt = min(n_win + 1, nb)
    first = jnp.clip(n - (n_wt - 1), 0, nb - n_wt)
    win_rows = pl.ds(pl.multiple_of(first * qb, qb), n_wt * qb)

    for gi in range(n_heads):
        qt, o_cmp, _ = heads[gi]

        def win_tile(i):
            j = first + i
            which = jnp.where(j > n, WB_NEG,
                              jnp.where(j == n, WB_CAUSAL, jnp.where(j == n - n_win, WB_LOWER, WB_ZERO)))
            rows = pl.ds(pl.multiple_of(j * qb, qb), qb)
            return _dot(kw_ref[0, gi, rows, :], qt) + wbias_ref[which]

        o_win = _softmax_pv(win_tile, n_wt, qb, s_ref.at[gi], p_ref.at[gi], vwt_ref[0, gi, :, win_rows])

        def gate(branch):
            rows = gt_ref[0, 0, pl.ds(branch * NSA_HEADS + (g0 + gi) * NSA_GROUP, NSA_GROUP), :]
            return jnp.concatenate([rows[r:r + 1, :] for r in range(NSA_GROUP)], axis=1)

        o = gate(0) * o_cmp + gate(1) * oslc_ref[gi] + gate(2) * o_win
        for r in range(NSA_GROUP):
            r0 = (gi * NSA_GROUP + r) * HEAD_DIM
            o_ref[0, 0, r0:r0 + HEAD_DIM, :] = o[:, r * qb:(r + 1) * qb]


def nsa_prompt_attention(qt, gt, kc, vct, ks, vst, kw, vwt):
    bsz, nb, g_, hd, width = qt.shape
    t = kw.shape[2]
    assert t % NSA_KT == 0
    ncp = kc.shape[2]
    n_cmp = (t - CMP_LEN) // CMP_STRIDE + 1
    n_slc = -(-t // SLC_BLOCK)
    nsp = -(-n_slc // BF16_ROWS) * BF16_ROWS
    cmap = jnp.asarray(_cmap_t(n_cmp, n_slc, nsp, ncp), BF16)
    win_bias, slc_bias = _attn_bias_consts()
    gs = NSA_HEADS_PER_STEP
    per_bg = lambda *shape: pl.BlockSpec((1, gs) + shape, lambda b, g, n: (b, g, 0, 0))
    const = lambda a: pl.BlockSpec(a.shape, lambda b, g, n: (0,) * a.ndim)
    return pl.pallas_call(
        functools.partial(_nsa_prompt_attn_kernel, n_cmp=n_cmp, n_slc=n_slc, nb=nb),
        grid=(bsz, g_ // gs, nb),
        in_specs=[pl.BlockSpec((1, 1, gs, hd, width), lambda b, g, n: (b, n, g, 0, 0)),
                  pl.BlockSpec((1, 1, NSA_GATE_ROWS, NSA_QB), lambda b, g, n: (b, n, 0, 0)),
                  per_bg(ncp, hd), per_bg(hd, ncp),
                  per_bg(t, 2 * hd), per_bg(hd, t), per_bg(t, hd), per_bg(hd, t),
                  const(cmap), const(win_bias), const(slc_bias)],
        out_specs=pl.BlockSpec((1, 1, gs * NSA_GROUP * hd, NSA_QB), lambda b, g, n: (b, n, g, 0)),
        out_shape=jax.ShapeDtypeStruct((bsz, nb, Q_WIDTH, NSA_QB), F32),
        scratch_shapes=[pltpu.VMEM((gs, t, width), F32), pltpu.VMEM((gs, t, width), BF16),
                        pltpu.VMEM((gs, hd, width), F32)],
        compiler_params=_params("parallel", "parallel", "arbitrary"),
        name="nsa_prompt_attention",
    )(qt, gt, kc, vct, ks, vst, kw, vwt, cmap, win_bias, slc_bias)


def _proj_t_deepnorm_kernel(at_ref, w_ref, x_ref, g_ref, b_ref, o_ref):
    a = at_ref[0, 0].T.astype(BF16)
    o_ref[0] = _deepnorm_rows(x_ref[0], _dot(a, w_ref[...]), g_ref[...], b_ref[...])


def proj_t_deepnorm(a_t, w_bf, x, g, b):
    bsz, nb, k, qb = a_t.shape
    d = x.shape[2]
    return pl.pallas_call(
        _proj_t_deepnorm_kernel,
        grid=(bsz, nb),
        in_specs=[pl.BlockSpec((1, 1, k, qb), lambda b, n: (b, n, 0, 0)),
                  pl.BlockSpec((k, d), lambda b, n: (0, 0)),
                  pl.BlockSpec((1, qb, d), lambda b, n: (b, n, 0)),
                  pl.BlockSpec((1, d), lambda b, n: (0, 0)),
                  pl.BlockSpec((1, d), lambda b, n: (0, 0))],
        out_specs=pl.BlockSpec((1, qb, d), lambda b, n: (b, n, 0)),
        out_shape=jax.ShapeDtypeStruct(x.shape, F32),
        compiler_params=_params("parallel", "parallel"),
        name="proj_t_deepnorm",
    )(a_t, w_bf, x, g.reshape(1, d), b.reshape(1, d))


def nsa_prompt_layer(x, w_in, b_gate, w_cmp, pe_cmp, w_out, g, b, layer, n_layers, prev_pages):
    xc, ks, kw, qt, vst, vwt, gt, kvc, kvs, kvw = nsa_prompt_proj(x, w_in, b_gate, 512, layer, n_layers, prev_pages)
    kc, vct = nsa_compress_prompt(xc, w_cmp, pe_cmp)
    o_t = nsa_prompt_attention(qt, gt, kc, vct, ks, vst, kw, vwt)
    return proj_t_deepnorm(o_t, w_out.astype(BF16), x, g, b), (kvc, kvs, kvw)


SAMPLE_SEQS_PER_STEP = 2


def _page_specs(cache_t, layer, n_seq, n_pages):
    block = (1, 1) + cache_t.shape[2:]
    return [pl.BlockSpec(block, lambda b, pt, si=si, k=k: (layer, pt[b * n_seq + si, k], 0, 0, 0, 0))
            for si in range(n_seq) for k in range(n_pages)]


def _nsa_compress_sample_kernel(*refs, n_seq, n_pages, n_cmp):
    page_refs = refs[1:1 + n_seq * n_pages]
    wk_ref, wv_ref, pe_ref, wflat_ref, wflat_t_ref, perm_ref, kct_ref, vct_ref, rows_ref = refs[1 + n_seq * n_pages:]
    kw = KV_WIDTH
    hb_page = PAGE_SIZE // CMP_STRIDE
    for k in range(n_seq * n_pages):
        for kv in range(2):
            tile = page_refs[k][0, 0, kv].reshape(kw, PAGE_SIZE).astype(BF16)
            by_p = _dot(tile, perm_ref[...]).T
            for p in range(CMP_STRIDE):
                rows_ref[kv, p, k * hb_page:(k + 1) * hb_page, :] = by_p[p * hb_page:(p + 1) * hb_page, :]
    period = n_pages * hb_page
    slabs_kv = kw // LANES
    load = lambda p, s: rows_ref[s // slabs_kv, p, :, (s % slabs_kv) * LANES:(s % slabs_kv + 1) * LANES]
    k_t, v_t = _compress(load, n_seq * period, period, n_cmp, wk_ref, wv_ref, pe_ref, wflat_ref, wflat_t_ref, False)
    for si in range(n_seq):
        kct_ref[si] = k_t[:, si * period:(si + 1) * period].astype(BF16)
        vct_ref[si] = v_t[:, si * period:(si + 1) * period].astype(BF16)


def nsa_compress_sample(cache_cmp_t, layer, page_table, w_cmp, pe_cmp):
    bsz, n_pages = page_table.shape
    nhb = n_pages * PAGE_SIZE // CMP_STRIDE
    n_cmp = (n_pages * PAGE_SIZE - CMP_LEN) // CMP_STRIDE + 1
    ops, specs = _cmp_operands(w_cmp, pe_cmp, False)
    tok = np.arange(PAGE_SIZE)
    perm = np.zeros((PAGE_SIZE, PAGE_SIZE), np.float32)
    perm[tok, (tok % CMP_STRIDE) * (PAGE_SIZE // CMP_STRIDE) + tok // CMP_STRIDE] = 1.0
    ops = ops + (jnp.asarray(perm, BF16),)
    specs = specs + [pl.BlockSpec(perm.shape, lambda *_: (0, 0))]
    n_seq = SAMPLE_SEQS_PER_STEP
    page_specs = _page_specs(cache_cmp_t, layer, n_seq, n_pages)
    out = jax.ShapeDtypeStruct((bsz, KV_WIDTH, nhb), BF16)
    out_spec = pl.BlockSpec((n_seq, KV_WIDTH, nhb), lambda b, pt: (b, 0, 0))
    return pl.pallas_call(
        functools.partial(_nsa_compress_sample_kernel, n_seq=n_seq, n_pages=n_pages, n_cmp=n_cmp),
        grid_spec=pltpu.PrefetchScalarGridSpec(
            num_scalar_prefetch=1, grid=(bsz // n_seq,),
            in_specs=page_specs + specs, out_specs=(out_spec, out_spec),
            scratch_shapes=[pltpu.VMEM((2, CMP_STRIDE, n_seq * nhb, KV_WIDTH), F32)]),
        out_shape=(out, out),
        compiler_params=_params("parallel"),
        name="nsa_compress_sample",
    )(page_table, *([cache_cmp_t] * (n_seq * n_pages)), *ops)


def _nsa_sample_attn_kernel(pt_ref, q_ref, kvn_ref, gp_ref, bg_ref, kct_ref, vct_ref, win_ref,
                            cmap_ref, g16_ref, e_ref, *rest, past, n_cmp, n_slc):
    del pt_ref
    all_pages, o_ref = rest[:-1], rest[-1]
    n_seq = q_ref.shape[0]
    n_pages = len(all_pages) // n_seq
    for si in range(n_seq):
        _nsa_sample_attn_one(si, q_ref, kvn_ref, gp_ref, bg_ref, kct_ref, vct_ref, win_ref, cmap_ref, g16_ref,
                             e_ref, all_pages[si * n_pages:(si + 1) * n_pages], o_ref, past, n_cmp, n_slc)


def _nsa_sample_attn_one(si, q_ref, kvn_ref, gp_ref, bg_ref, kct_ref, vct_ref, win_ref, cmap_ref, g16_ref,
                         e_ref, page_refs, o_ref, past, n_cmp, n_slc):
    kw = KV_WIDTH
    q16 = q_ref[si] * ATTN_SCALE
    rowg = lax.broadcasted_iota(jnp.int32, (NSA_HEADS, 1), 0) // NSA_GROUP
    colg = lax.broadcasted_iota(jnp.int32, (1, kw), 1) // HEAD_DIM
    own = rowg == colg
    qbd = jnp.where(own, jnp.concatenate([q16] * NSA_KV_HEADS, axis=1), 0.0).astype(BF16)

    def pick(o_full):
        om = jnp.where(own, o_full, 0.0)
        out = om[:, 0:HEAD_DIM]
        for g in range(1, NSA_KV_HEADS):
            out = out + om[:, g * HEAD_DIM:(g + 1) * HEAD_DIM]
        return out

    def new_row(lo):
        return kvn_ref[si][:, lo:lo + kw]

    def attend(k_t, v_t, mask, k_new, v_new, new_ok):
        s = jnp.where(mask, _dot(qbd, k_t), NEG_INF)
        m = jnp.max(s, axis=1, keepdims=True)
        if k_new is not None:
            k8 = jnp.broadcast_to(k_new.astype(BF16), (8, kw))
            s_new = jnp.where(new_ok, _dot_nt(qbd, k8)[:, 0:1], NEG_INF)
            m = jnp.maximum(m, s_new)
        e = jnp.where(mask, jnp.exp(s - m), 0.0)
        l = jnp.sum(e, axis=1, keepdims=True)
        o_full = _dot_nt(e.astype(BF16), v_t)
        if k_new is not None:
            e_new = jnp.where(new_ok, jnp.exp(s_new - m), 0.0)
            l = l + e_new
            o_full = o_full + e_new * v_new
        l = jnp.where(l > 0.0, l, 1.0)
        return pick(o_full) / l, e / l

    q_pos = past
    ncp = kct_ref.shape[2]
    c = lax.broadcasted_iota(jnp.int32, (1, ncp), 1)
    cmask = (c * CMP_STRIDE + (CMP_LEN - 1) <= q_pos) & (c < n_cmp)
    o_cmp, p_cmp = attend(kct_ref[si], vct_ref[si], cmask, None, None, None)

    psum = sum(_dot(g16_ref[...], piece) for piece in _split3(p_cmp))
    imp = sum(_dot(piece, cmap_ref[...]) for piece in _split3(psum))
    nsp = cmap_ref.shape[1]
    blk = lax.broadcasted_iota(jnp.int32, (1, nsp), 1)
    cur = q_pos // SLC_BLOCK
    forced = (blk == 0) | (blk == cur) | (blk == cur - 1)
    allowed = (blk <= cur) & (blk < n_slc)
    score = jnp.where(allowed, imp + jnp.where(forced, FORCE_BONUS, 0.0), -jnp.inf)
    sel = (_rank_before(score, n_slc, 1, blk) < min(TOP_N, n_slc)) & allowed
    sel_keys = _dot(sel.astype(BF16), e_ref[...]) > 0.5

    k_t = jnp.concatenate([r[0, 0, 0].reshape(kw, PAGE_SIZE) for r in page_refs], axis=1).astype(BF16)
    v_t = jnp.concatenate([r[0, 0, 1].reshape(kw, PAGE_SIZE) for r in page_refs], axis=1).astype(BF16)
    new_ok = sel[:, cur:cur + 1]
    o_slc, _ = attend(k_t, v_t, sel_keys, new_row(2 * kw), new_row(3 * kw), new_ok)

    wb = win_ref.shape[-1]
    wpos = past - wb + lax.broadcasted_iota(jnp.int32, (1, wb), 1)
    wmask = (wpos <= q_pos) & (wpos > q_pos - WINDOW) & (wpos >= 0)
    o_win, _ = attend(win_ref[0, si, 0].reshape(kw, wb).astype(BF16), win_ref[0, si, 1].reshape(kw, wb).astype(BF16),
                      wmask, new_row(4 * kw), new_row(5 * kw), True)

    gates = jax.nn.sigmoid(gp_ref[si] + bg_ref[...])
    o_ref[si] = gates[:, 0:1] * o_cmp + gates[:, 1:2] * o_slc + gates[:, 2:3] * o_win


def nsa_sample_layer(x, cache_cmp_t, cache_slc_t, cache_win_t, layer, page_table, w_in, b_gate, w_cmp, pe_cmp,
                     w_out, g, b):
    bsz, d = x.shape
    n_pages = page_table.shape[1]
    past = n_pages * PAGE_SIZE
    w2 = 2 * KV_WIDTH
    n_in = w_in.shape[1]
    n_pad = -(-n_in // LANES) * LANES
    w_pad = jnp.concatenate([w_in, jnp.zeros((d, n_pad - n_in), w_in.dtype)], axis=1).astype(BF16)
    proj = linear(x, w_pad, tm=bsz, tn=n_pad)
    q = proj[:, :Q_WIDTH].reshape(bsz, NSA_HEADS, HEAD_DIM)
    kv_new = proj[:, Q_WIDTH:Q_WIDTH + 3 * w2]
    gate_pre = proj[:, Q_WIDTH + 3 * w2:n_in].reshape(bsz, 3, NSA_HEADS).transpose(0, 2, 1)
    bg = b_gate.reshape(3, NSA_HEADS).T

    kct, vct = nsa_compress_sample(cache_cmp_t, layer, page_table, w_cmp, pe_cmp)
    ncp = kct.shape[2]
    n_cmp = (past - CMP_LEN) // CMP_STRIDE + 1
    assert (past + 1 - CMP_LEN) // CMP_STRIDE + 1 == n_cmp
    n_slc = -(-(past + 1) // SLC_BLOCK)
    nsp = -(-n_slc // LANES) * LANES
    cmap = jnp.asarray(_cmap_t(n_cmp, n_slc, nsp, ncp).T, BF16)
    heads = np.arange(NSA_HEADS)
    g16 = jnp.asarray(heads[:, None] // NSA_GROUP == heads[None, :] // NSA_GROUP, BF16)
    expand = jnp.asarray(np.arange(nsp)[:, None] == np.arange(past)[None, :] // SLC_BLOCK, BF16)

    wb = cache_win_t.shape[-1]
    const = lambda a: pl.BlockSpec(a.shape, lambda b, pt: (0,) * a.ndim)
    n_seq = SAMPLE_SEQS_PER_STEP
    per_b = lambda *shape: pl.BlockSpec((n_seq,) + shape, lambda b, pt: (b,) + (0,) * len(shape))
    page_specs = _page_specs(cache_slc_t, layer, n_seq, n_pages)
    win_spec = pl.BlockSpec((1, n_seq) + cache_win_t.shape[2:], lambda b, pt: (layer, b, 0, 0, 0, 0))
    o = pl.pallas_call(
        functools.partial(_nsa_sample_attn_kernel, past=past, n_cmp=n_cmp, n_slc=n_slc),
        grid_spec=pltpu.PrefetchScalarGridSpec(
            num_scalar_prefetch=1, grid=(bsz // n_seq,),
            in_specs=[per_b(NSA_HEADS, HEAD_DIM), per_b(1, 3 * w2), per_b(NSA_HEADS, 3), const(bg),
                      per_b(KV_WIDTH, ncp), per_b(KV_WIDTH, ncp), win_spec,
                      const(cmap), const(g16), const(expand)] + page_specs,
            out_specs=per_b(NSA_HEADS, HEAD_DIM)),
        out_shape=jax.ShapeDtypeStruct((bsz, NSA_HEADS, HEAD_DIM), F32),
        compiler_params=_params("parallel"),
        name="nsa_sample_attention",
    )(page_table, q, kv_new.reshape(bsz, 1, 3 * w2), gate_pre, bg, kct, vct, cache_win_t, cmap, g16, expand,
      *([cache_slc_t] * (n_seq * n_pages)))
    y = proj_deepnorm(o.reshape(bsz, Q_WIDTH), w_out.astype(BF16), x, g, b, tm=bsz)
    return y, kv_new


HGRN_C = 128
HGRN_SUB = 8
HGRN_LEVELS = (8, 16, 32, 64)


def _hgrn_lower_bound(logits, layer):
    e = jnp.exp(logits - jnp.max(logits, axis=0, keepdims=True))
    sm = e / jnp.sum(e, axis=0, keepdims=True)
    lb = jnp.zeros_like(sm[0])
    for j in range(1, layer + 1):
        lb = lb + sm[j]
    return lb


def _hgrn_gates(f_raw, lb):
    sg = jax.nn.sigmoid(f_raw)
    f = lb + (1.0 - lb) * sg
    log_f = jnp.where(f > 0.0, jnp.log(f), jnp.log1p(-lb) + f_raw)
    return log_f, (1.0 - lb) * (1.0 - sg)


def _rms_gate(o, gain, g_raw):
    o = o * lax.rsqrt(jnp.mean(o * o, axis=-1, keepdims=True) + LN_EPS) * gain
    return o * (g_raw * jax.nn.sigmoid(g_raw))


def _hgrn_prompt_kernel(x_ref, wq_ref, wf_ref, wi_ref, wg_ref, lb_ref, gain_ref, tri_ref, lmask_ref,
                        y_ref, s_ref, st_ref, cum_ref, *, layer):
    c = HGRN_C
    nblk = c // HGRN_SUB
    lb_all = _hgrn_lower_bound(lb_ref[:, 0], layer)
    st_ref[...] = jnp.zeros_like(st_ref)
    rsub = lax.broadcasted_iota(jnp.int32, (c, 1), 0) % HGRN_SUB

    def roll8(a, j):
        return pltpu.roll(a.reshape(nblk, HGRN_SUB, a.shape[1]), j, 1).reshape(a.shape)

    def head_chunk(hi, rows, q, f_raw, v, g_raw):
        lf, k = _hgrn_gates(f_raw, lb_all[:, hi * HGRN_DK:(hi + 1) * HGRN_DK])
        cum = sum(_dot(tri_ref[...], piece) for piece in _split3(lf))
        cum_ref[hi] = cum
        ends = cum_ref[hi, pl.ds(HGRN_SUB - 1, nblk, stride=HGRN_SUB), :]
        end_b = [jnp.broadcast_to(ends[i:i + 1, :], (HGRN_SUB, HGRN_DK)) for i in range(nblk)]
        zero_b = jnp.zeros((HGRN_SUB, HGRN_DK), F32)
        vb = v.astype(BF16)

        a_off = jnp.zeros((c, c), F32)
        for li, lev in enumerate(HGRN_LEVELS):
            per = lev // HGRN_SUB
            before = [end_b[(i // per) * per - 1] if i >= per else zero_b for i in range(nblk)]
            after = [end_b[(i // per) * per + per - 1] for i in range(nblk)]
            qd = q * jnp.exp(cum - jnp.concatenate(before, axis=0))
            kd = k * jnp.exp(jnp.concatenate(after, axis=0) - cum)
            a_off = a_off + _dot_nt(qd.astype(BF16), kd.astype(BF16)) * lmask_ref[li]
        o = _dot(a_off.astype(BF16), vb)

        for j in range(HGRN_SUB):
            if j == 0:
                w = q * k
                vj = v
            else:
                ok = rsub >= j
                w = jnp.where(ok, q * roll8(k, j) * jnp.exp(cum - roll8(cum, j)), 0.0)
                vj = roll8(v, j)
            o = o + jnp.sum(w, axis=1, keepdims=True) * vj

        last = end_b[nblk - 1][0:1, :]
        o = o + _dot_nt((q * jnp.exp(cum)).astype(BF16), st_ref[hi].astype(BF16))
        kd = (k * jnp.exp(last - cum)).astype(BF16)
        st_ref[hi] = st_ref[hi] * jnp.exp(last) + _dot(v.T.astype(BF16), kd)
        y_ref[0, rows, hi * HGRN_DV:(hi + 1) * HGRN_DV] = _rms_gate(o, gain_ref[...], g_raw)

    n_heads = st_ref.shape[0]

    def chunk(ci, _):
        rows = pl.ds(pl.multiple_of(ci * c, c), c)
        xb = x_ref[0, rows, :].astype(BF16)
        q, f_raw, v, g_raw = (_dot(xb, w[...]) for w in (wq_ref, wf_ref, wi_ref, wg_ref))
        for hi in range(n_heads):
            cols = slice(hi * HGRN_DK, (hi + 1) * HGRN_DK)
            head_chunk(hi, rows, q[:, cols], f_raw[:, cols], v[:, cols], g_raw[:, cols])
        return 0

    lax.fori_loop(0, x_ref.shape[1] // c, chunk, 0)
    for hi in range(n_heads):
        s_ref[0, hi] = st_ref[hi].T


HGRN_HEADS_PER_STEP = 2


def hgrn_prompt(x, w_in_bf, lb_logits, gain, layer):
    bsz, t, d = x.shape
    c = HGRN_C
    hh = HGRN_HEADS
    hps = HGRN_HEADS_PER_STEP
    wide = hps * HGRN_DK
    nl = lb_logits.shape[0]
    tri = jnp.asarray(np.tril(np.ones((c, c), np.float32)), BF16)
    ti = np.arange(c)[:, None]
    si = np.arange(c)[None, :]
    lmask = jnp.asarray(np.stack([((ti // (2 * lev) == si // (2 * lev)) & ((ti // lev) % 2 == 1)
                                   & ((si // lev) % 2 == 0)) for lev in HGRN_LEVELS]), F32)
    w_cols = lambda part: pl.BlockSpec((d, wide), lambda b, h: (0, part * (hh // hps) + h))
    return pl.pallas_call(
        functools.partial(_hgrn_prompt_kernel, layer=layer),
        grid=(bsz, hh // hps),
        in_specs=[pl.BlockSpec((1, t, d), lambda b, h: (b, 0, 0)),
                  w_cols(0), w_cols(1), w_cols(2), w_cols(3),
                  pl.BlockSpec((nl, 1, 1, wide), lambda b, h: (0, h, 0, 0)),
                  pl.BlockSpec((1, HGRN_DV), lambda b, h: (0, 0)),
                  pl.BlockSpec((c, c), lambda b, h: (0, 0)),
                  pl.BlockSpec((len(HGRN_LEVELS), c, c), lambda b, h: (0, 0, 0))],
        out_specs=(pl.BlockSpec((1, t, hps * HGRN_DV), lambda b, h: (b, 0, h)),
                   pl.BlockSpec((1, hps, HGRN_DK, HGRN_DV), lambda b, h: (b, h, 0, 0))),
        out_shape=(jax.ShapeDtypeStruct((bsz, t, hh * HGRN_DV), F32),
                   jax.ShapeDtypeStruct((bsz, hh, HGRN_DK, HGRN_DV), F32)),
        scratch_shapes=[pltpu.VMEM((hps, HGRN_DV, HGRN_DK), F32), pltpu.VMEM((hps, c, HGRN_DK), F32)],
        compiler_params=_params("parallel", "arbitrary"),
        name="hgrn_prompt",
    )(x, w_in_bf, w_in_bf, w_in_bf, w_in_bf, lb_logits.reshape(nl, hh // hps, 1, wide),
      gain.reshape(1, HGRN_DV), tri, lmask)


def _hgrn_sample_kernel(p_ref, s0_ref, lb_ref, gain_ref, y_ref, s_ref, *, layer):
    hh = HGRN_HEADS
    n_seq = p_ref.shape[0]
    lb = _hgrn_lower_bound(lb_ref[...], layer)
    parts, rows = [], []
    for si in range(n_seq):
        p = p_ref[si]
        q8, f8, i8, g8 = (p[j * hh:(j + 1) * hh, :] for j in range(4))
        lf8, k8 = _hgrn_gates(f8, lb)
        parts.append((q8, k8, i8, g8))
        rows.append(jnp.exp(lf8))
    pad = [jnp.zeros((LANES - hh * n_seq, HGRN_DK), F32)] if hh * n_seq < LANES else []
    cols = jnp.concatenate(rows + pad, axis=0).T
    zeros = jnp.zeros((BF16_ROWS - 1, HGRN_DK), BF16)
    head_row = lax.broadcasted_iota(jnp.int32, (hh, 1), 0)
    for si in range(n_seq):
        q8, k8, i8, g8 = parts[si]
        q8b = q8.astype(BF16)
        for h in range(hh):
            k16 = jnp.concatenate([k8[h:h + 1, :].astype(BF16), zeros], axis=0)
            i16 = jnp.concatenate([i8[h:h + 1, :].astype(BF16), zeros], axis=0)
            outer = lax.dot_general(k16, i16, (((0,), (0,)), ((), ())), preferred_element_type=F32)
            s_new = cols[:, hh * si + h:hh * si + h + 1] * s0_ref[si, h] + outer
            s_ref[si, h] = s_new
            o_h = _dot(q8b, s_new.astype(BF16))
            o = o_h if h == 0 else jnp.where(head_row == h, o_h, o)
        y_ref[si] = _rms_gate(o, gain_ref[...], g8)


HGRN_SAMPLE_SEQS = 8


def hgrn_sample(proj, s0_all, j, lb_logits, gain, layer):
    bsz = proj.shape[0]
    hh = HGRN_HEADS
    nl = lb_logits.shape[0]
    n_seq = HGRN_SAMPLE_SEQS
    assert bsz % n_seq == 0
    y, s_new = pl.pallas_call(
        functools.partial(_hgrn_sample_kernel, layer=layer),
        grid=(bsz // n_seq,),
        in_specs=[pl.BlockSpec((n_seq, 4 * hh, HGRN_DK), lambda b: (b, 0, 0)),
                  pl.BlockSpec((None, n_seq, hh, HGRN_DK, HGRN_DV), lambda b: (j, b, 0, 0, 0)),
                  pl.BlockSpec((nl, hh, HGRN_DK), lambda b: (0, 0, 0)),
                  pl.BlockSpec((1, HGRN_DV), lambda b: (0, 0))],
        out_specs=(pl.BlockSpec((n_seq, hh, HGRN_DV), lambda b: (b, 0, 0)),
                   pl.BlockSpec((n_seq, hh, HGRN_DK, HGRN_DV), lambda b: (b, 0, 0, 0))),
        out_shape=(jax.ShapeDtypeStruct((bsz, hh, HGRN_DV), F32),
                   jax.ShapeDtypeStruct(s0_all.shape[1:], F32)),
        compiler_params=_params("parallel"),
        name="hgrn_sample",
    )(proj.reshape(bsz, 4 * hh, HGRN_DK), s0_all, lb_logits.reshape(nl, hh, HGRN_DK), gain.reshape(1, HGRN_DV))
    return y.reshape(bsz, hh * HGRN_DV), s_new


TM_PROJ = 512
TM_FFN = 1024
FFN_HC = 256
TM_GMLP = 256
TM_LINEAR = 512
TN_LINEAR = 1024


def kernel(x_prompt, x_sample, cache_cmp_kv, cache_slc_kv, cache_win_kv, state_hgrn, page_table, ln_gain, ln_bias, ffn_w_in, ffn_w_out, nsa_w_in, nsa_b_gate, nsa_w_cmp, nsa_pe_cmp, nsa_w_out, gmlp_w_in, gmlp_b_in, gmlp_ln_v, gmlp_w_sp, gmlp_b_sp, gmlp_w_out, hgrn_w_in, hgrn_lb_logits, hgrn_norm_gain, hgrn_w_out):
    bsz, t, d = x_prompt.shape
    sb = x_sample.shape[0]
    assert x_sample.shape[1] == 1
    xp = x_prompt
    xs = x_sample.reshape(sb, d)
    kv_shape = (2, NSA_KV_HEADS, HEAD_DIM)
    wb_prompt = min(WINDOW, t)
    cmp_t, slc_t, win_t = _token_minor(cache_cmp_kv), _token_minor(cache_slc_kv), _token_minor(cache_win_kv)
    cmp_s, slc_s, win_s, gv_s, hs_p, hs_s = [], [], [], [], [], []
    n_nsa = (DEPTH + N_MIXERS - 1) // N_MIXERS
    kv_pages = None
    ffn_in_bf, ffn_out_bf = ffn_w_in.astype(BF16), ffn_w_out.astype(BF16)
    for layer in range(DEPTH):
        kind, j = layer % N_MIXERS, layer // N_MIXERS
        g0, b0 = ln_gain[layer, 0], ln_bias[layer, 0]
        if kind == 0:
            xp, kv_pages = nsa_prompt_layer(xp, nsa_w_in[j], nsa_b_gate[j], nsa_w_cmp[j], nsa_pe_cmp[j],
                                            nsa_w_out[j], g0, b0, j, n_nsa, kv_pages)
            xs, kv_new = nsa_sample_layer(xs, cmp_t, slc_t, win_t, j, page_table,
                                          nsa_w_in[j], nsa_b_gate[j], nsa_w_cmp[j], nsa_pe_cmp[j], nsa_w_out[j],
                                          g0, b0)
            w2 = 2 * KV_WIDTH
            cmp_s.append(kv_new[:, 0:w2].reshape((sb, 1) + kv_shape))
            slc_s.append(kv_new[:, w2:2 * w2].reshape((sb, 1) + kv_shape))
            win_s.append(kv_new[:, 2 * w2:3 * w2].reshape((sb, 1) + kv_shape))
        elif kind == 1:
            w_in_bf, w_out_bf = gmlp_w_in[j].astype(BF16), gmlp_w_out[j].astype(BF16)
            xp = gmlp_layer(xp.reshape(bsz * t, d), w_in_bf, gmlp_b_in[j], gmlp_ln_v[j], gmlp_w_sp[j], gmlp_b_sp[j],
                            w_out_bf, g0, b0, chunked=True, tm=TM_GMLP).reshape(bsz, t, d)
            xs, v_new = gmlp_layer(xs, w_in_bf, gmlp_b_in[j], gmlp_ln_v[j], gmlp_w_sp[j], gmlp_b_sp[j],
                                   w_out_bf, g0, b0, chunked=False, tm=sb)
            gv_s.append(v_new.reshape(sb, 1, GMLP_WIDTH))
        else:
            w_in_bf, w_out_bf = hgrn_w_in[j].astype(BF16), hgrn_w_out[j].astype(BF16)
            y, s_p = hgrn_prompt(xp, w_in_bf, hgrn_lb_logits, hgrn_norm_gain[j], layer)
            xp = proj_deepnorm(y.reshape(bsz * t, HGRN_WIDTH), w_out_bf, xp.reshape(bsz * t, d), g0, b0,
                               TM_PROJ).reshape(bsz, t, d)
            proj_s = linear(xs, w_in_bf, sb, TN_LINEAR)
            y_s, s_s = hgrn_sample(proj_s, state_hgrn, j, hgrn_lb_logits, hgrn_norm_gain[j], layer)
            xs = proj_deepnorm(y_s, w_out_bf, xs, g0, b0, sb)
            hs_p.append(s_p)
            hs_s.append(s_s)
        g1, b1 = ln_gain[layer, 1], ln_bias[layer, 1]
        xp = ffn_deepnorm(xp.reshape(bsz * t, d), ffn_in_bf, ffn_out_bf, layer, g1, b1, TM_FFN, FFN_HC).reshape(bsz, t, d)
        xs = ffn_deepnorm(xs, ffn_in_bf, ffn_out_bf, layer, g1, b1, sb, FFN_HC)
    kvc, kvs, kvw = kv_pages
    n_page = t // PAGE_SIZE
    pages = lambda a: _token_major(a.reshape((n_nsa, bsz * n_page) + a.shape[3:]))
    win_rows = _token_major(kvw[:, :, n_page - wb_prompt // PAGE_SIZE:])
    win_p = win_rows.reshape((n_nsa, bsz, wb_prompt) + kv_shape)
    return (xp, xs.reshape(sb, 1, d), pages(kvc), jnp.stack(cmp_s), pages(kvs), jnp.stack(slc_s),
            win_p, jnp.stack(win_s), jnp.stack(gv_s), jnp.stack(hs_p), jnp.stack(hs_s))
```

```python
import functools

import jax
import jax.numpy as jnp
import numpy as np
from jax import lax
from jax.experimental import pallas as pl
from jax.experimental.pallas import tpu as pltpu

F32 = jnp.float32
BF16 = jnp.bfloat16

D_MODEL = 1024
DEPTH = 4
PAGE_SIZE = 128
N_MIXERS = 3

NSA_HEADS = 16
NSA_KV_HEADS = 4
NSA_GROUP = NSA_HEADS // NSA_KV_HEADS
HEAD_DIM = D_MODEL // NSA_HEADS
Q_WIDTH = NSA_HEADS * HEAD_DIM
KV_WIDTH = NSA_KV_HEADS * HEAD_DIM
CMP_LEN = 32
CMP_STRIDE = 16
SLC_BLOCK = 64
TOP_N = 16
WINDOW = 512
FORCE_BONUS = 1e3
ATTN_SCALE = HEAD_DIM ** -0.5
NEG_INF = -1e30

GMLP_WIDTH = D_MODEL
GMLP_GROUPS = 8
GMLP_GROUP_DIM = GMLP_WIDTH // GMLP_GROUPS
GMLP_CHUNK = 128

HGRN_HEADS = 8
HGRN_DK = 128
HGRN_DV = 128
HGRN_WIDTH = HGRN_HEADS * HGRN_DK

FFN_HIDDEN = 2816
DEEPNORM_ALPHA = (2 * DEPTH) ** 0.25
LN_EPS = 1e-5

LANES = 128
BF16_ROWS = 16
VMEM_LIMIT = 48 * 1024 * 1024


def _params(*sem):
    return pltpu.CompilerParams(dimension_semantics=sem, vmem_limit_bytes=VMEM_LIMIT)


def _dot(a, b):
    return jnp.dot(a, b, preferred_element_type=F32)


def _dot_nt(a, b):
    return lax.dot_general(a, b, (((1,), (1,)), ((), ())), preferred_element_type=F32)


def _layer_norm_rows(y, g, b):
    mu = jnp.mean(y, axis=-1, keepdims=True)
    yc = y - mu
    var = jnp.mean(yc * yc, axis=-1, keepdims=True)
    return yc * lax.rsqrt(var + LN_EPS) * g + b


def _deepnorm_rows(x, h, g, b):
    return _layer_norm_rows(DEEPNORM_ALPHA * x + h, g, b)


def _split3(x):
    hi = x.astype(BF16)
    r1 = x - hi.astype(F32)
    mid = r1.astype(BF16)
    lo = (r1 - mid.astype(F32)).astype(BF16)
    return hi, mid, lo


def _proj_deepnorm_kernel(a_ref, w_ref, x_ref, g_ref, b_ref, o_ref):
    h = _dot(a_ref[...].astype(BF16), w_ref[...])
    o_ref[...] = _deepnorm_rows(x_ref[...], h, g_ref[...], b_ref[...])


def proj_deepnorm(a, w_bf, x, g, b, tm):
    m, k = a.shape
    d = x.shape[1]
    tm = min(tm, m)
    return pl.pallas_call(
        _proj_deepnorm_kernel,
        grid=(m // tm,),
        in_specs=[pl.BlockSpec((tm, k), lambda i: (i, 0)),
                  pl.BlockSpec((k, d), lambda i: (0, 0)),
                  pl.BlockSpec((tm, d), lambda i: (i, 0)),
                  pl.BlockSpec((1, d), lambda i: (0, 0)),
                  pl.BlockSpec((1, d), lambda i: (0, 0))],
        out_specs=pl.BlockSpec((tm, d), lambda i: (i, 0)),
        out_shape=jax.ShapeDtypeStruct((m, d), F32),
        compiler_params=_params("parallel"),
        name="proj_deepnorm",
    )(a, w_bf, x, g.reshape(1, d), b.reshape(1, d))


def _ffn_kernel(x_ref, wg_ref, wu_ref, wo_ref, g_ref, b_ref, o_ref, xb_ref, acc_ref):
    j = pl.program_id(1)

    @pl.when(j == 0)
    def _():
        xb_ref[...] = x_ref[...].astype(BF16)
        acc_ref[...] = jnp.zeros_like(acc_ref)

    xb = xb_ref[...]
    gate = _dot(xb, wg_ref[...])
    up = _dot(xb, wu_ref[...])
    mid = (gate * jax.nn.sigmoid(gate) * up).astype(BF16)
    acc_ref[...] += _dot(mid, wo_ref[...])

    @pl.when(j == pl.num_programs(1) - 1)
    def _():
        o_ref[...] = _deepnorm_rows(x_ref[...], acc_ref[...], g_ref[...], b_ref[...])


def ffn_deepnorm(x, w_in_bf, w_out_bf, layer, g, b, tm, hc):
    m, d = x.shape
    hidden = w_out_bf.shape[1]
    tm = min(tm, m)
    nh = hidden // hc
    return pl.pallas_call(
        _ffn_kernel,
        grid=(m // tm, nh),
        in_specs=[pl.BlockSpec((tm, d), lambda i, j: (i, 0)),
                  pl.BlockSpec((None, d, hc), lambda i, j: (layer, 0, j)),
                  pl.BlockSpec((None, d, hc), lambda i, j: (layer, 0, nh + j)),
                  pl.BlockSpec((None, hc, d), lambda i, j: (layer, j, 0)),
                  pl.BlockSpec((1, d), lambda i, j: (0, 0)),
                  pl.BlockSpec((1, d), lambda i, j: (0, 0))],
        out_specs=pl.BlockSpec((tm, d), lambda i, j: (i, 0)),
        out_shape=jax.ShapeDtypeStruct((m, d), F32),
        scratch_shapes=[pltpu.VMEM((tm, d), BF16), pltpu.VMEM((tm, d), F32)],
        compiler_params=_params("parallel", "arbitrary"),
        name="ffn_deepnorm",
    )(x, w_in_bf, w_in_bf, w_out_bf, g.reshape(1, d), b.reshape(1, d))


def _gelu_exact(z):
    return 0.5 * z * (1.0 + lax.erf(z * (0.5 ** 0.5)))


def _gmlp_kernel(x_ref, win_ref, bin_ref, lnv_ref, wsp_ref, bsp_ref, wout_ref, g_ref, b_ref,
                 o_ref, *rest, chunked):
    width = GMLP_WIDTH
    x = x_ref[...]
    z = _dot(x.astype(BF16), win_ref[...]) + bin_ref[...]
    gz = _gelu_exact(z)
    u = gz[:, :width]
    v = _layer_norm_rows(gz[:, width:], lnv_ref[0:1, :], lnv_ref[1:2, :])
    if chunked:
        mix_ref, = rest
        vb = v.astype(BF16)
        for c in range(x.shape[0] // GMLP_CHUNK):
            rows = slice(c * GMLP_CHUNK, (c + 1) * GMLP_CHUNK)
            for h in range(GMLP_GROUPS):
                cols = slice(h * GMLP_GROUP_DIM, (h + 1) * GMLP_GROUP_DIM)
                mix_ref[rows, cols] = _dot(wsp_ref[h], vb[rows, cols]) + bsp_ref[:, cols]
        mixed = mix_ref[...]
    else:
        v_ref, = rest
        v_ref[...] = v
        mixed = v * wsp_ref[...] + bsp_ref[...]
    h_out = _dot((u * mixed).astype(BF16), wout_ref[...])
    o_ref[...] = _deepnorm_rows(x, h_out, g_ref[...], b_ref[...])


def gmlp_layer(x, w_in_bf, b_in, ln_v, w_sp, b_sp, w_out_bf, g, b, *, chunked, tm):
    m, d = x.shape
    width = GMLP_WIDTH
    tm = min(tm, m)
    const = lambda *shape: pl.BlockSpec(shape, lambda i: (0,) * len(shape))
    if chunked:
        wsp = jnp.tril(w_sp).astype(BF16)
        bsp = jnp.repeat(b_sp.T, GMLP_GROUP_DIM, axis=1)
        wsp_spec, bsp_spec = const(GMLP_GROUPS, GMLP_CHUNK, GMLP_CHUNK), const(GMLP_CHUNK, width)
        out_shape = jax.ShapeDtypeStruct((m, d), F32)
        out_specs = pl.BlockSpec((tm, d), lambda i: (i, 0))
        scratch = [pltpu.VMEM((tm, width), F32)]
    else:
        wsp = jnp.repeat(w_sp[:, 0, 0], GMLP_GROUP_DIM).reshape(1, width)
        bsp = jnp.repeat(b_sp[:, 0], GMLP_GROUP_DIM).reshape(1, width)
        wsp_spec, bsp_spec = const(1, width), const(1, width)
        out_shape = (jax.ShapeDtypeStruct((m, d), F32), jax.ShapeDtypeStruct((m, width), F32))
        out_specs = (pl.BlockSpec((tm, d), lambda i: (i, 0)), pl.BlockSpec((tm, width), lambda i: (i, 0)))
        scratch = []
    return pl.pallas_call(
        functools.partial(_gmlp_kernel, chunked=chunked),
        grid=(m // tm,),
        in_specs=[pl.BlockSpec((tm, d), lambda i: (i, 0)),
                  const(d, 2 * width), const(1, 2 * width), const(2, width),
                  wsp_spec, bsp_spec, const(width, d), const(1, d), const(1, d)],
        out_specs=out_specs,
        out_shape=out_shape,
        scratch_shapes=scratch,
        compiler_params=_params("parallel"),
        name="gmlp_layer",
    )(x, w_in_bf, b_in.reshape(1, 2 * width), ln_v, wsp, bsp, w_out_bf, g.reshape(1, d), b.reshape(1, d))


def _linear_kernel(x_ref, w_ref, o_ref):
    o_ref[...] = _dot(x_ref[...].astype(BF16), w_ref[...])


def linear(x, w_bf, tm, tn):
    m, k = x.shape
    n = w_bf.shape[1]
    tm, tn = min(tm, m), min(tn, n)
    return pl.pallas_call(
        _linear_kernel,
        grid=(m // tm, n // tn),
        in_specs=[pl.BlockSpec((tm, k), lambda i, j: (i, 0)),
                  pl.BlockSpec((k, tn), lambda i, j: (0, j))],
        out_specs=pl.BlockSpec((tm, tn), lambda i, j: (i, j)),
        out_shape=jax.ShapeDtypeStruct((m, n), F32),
        compiler_params=_params("parallel", "parallel"),
        name="linear",
    )(x, w_bf)


def _token_minor(cache):
    nd = cache.ndim
    return jnp.transpose(cache, tuple(range(nd - 4)) + (nd - 3, nd - 2, nd - 1, nd - 4))


def _token_major(kvt):
    nd = kvt.ndim
    return jnp.transpose(kvt, tuple(range(nd - 4)) + (nd - 1, nd - 4, nd - 3, nd - 2))


CMP_SLABS = 2 * KV_WIDTH // LANES


def _cmp_weights(w_cmp):
    w = w_cmp.reshape(2, 2, CMP_STRIDE, HEAD_DIM, HEAD_DIM)
    eye = jnp.eye(NSA_KV_HEADS, dtype=w_cmp.dtype)
    wbd = jnp.einsum('gh,klpde->pkgdlhe', eye, w)
    return wbd.reshape(CMP_STRIDE, 2, KV_WIDTH, 2 * KV_WIDTH).astype(BF16)


def _cmp_operands(w_cmp, pe_cmp, k_natural):
    wbd = _cmp_weights(w_cmp)
    wk = wbd[:, 0] if k_natural else jnp.swapaxes(wbd[:, 0], 1, 2)
    wv = jnp.swapaxes(wbd[:, 1], 1, 2)
    pe = pe_cmp.reshape(2, CMP_LEN * HEAD_DIM)
    wflat = w_cmp.reshape(2, CMP_LEN * HEAD_DIM, HEAD_DIM).astype(BF16)
    wflat_t = jnp.swapaxes(wflat, 1, 2)
    const = lambda a: pl.BlockSpec(a.shape, lambda *_: (0,) * a.ndim)
    ops = (wk, wv, pe, wflat, wflat_t)
    return ops, [const(a) for a in ops]


def _compress(load_rows, nhb, period, n_cmp, wk_ref, wv_ref, pe_ref, wflat_ref, wflat_t_ref, k_natural):
    kw = KV_WIDTH
    acc_k = jnp.zeros((nhb, 2 * kw) if k_natural else (2 * kw, nhb), F32)
    acc_v = jnp.zeros((2 * kw, nhb), F32)
    for p in range(CMP_STRIDE):
        slabs = [load_rows(p, s).astype(BF16) for s in range(CMP_SLABS)]
        xk = jnp.concatenate(slabs[:CMP_SLABS // 2], axis=1)
        xv = jnp.concatenate(slabs[CMP_SLABS // 2:], axis=1)
        acc_k = acc_k + (_dot(xk, wk_ref[p]) if k_natural else _dot_nt(wk_ref[p], xk))
        acc_v = acc_v + _dot_nt(wv_ref[p], xv)

    def pe8(kv):
        return jnp.broadcast_to(pe_ref[kv:kv + 1, :], (8, CMP_LEN * HEAD_DIM)).astype(BF16)

    def finish_t(acc, kv):
        bias = _dot_nt(wflat_t_ref[kv], pe8(kv))[:, 0:1]
        bias = jnp.concatenate([bias] * NSA_KV_HEADS, axis=0)
        out = acc[:kw, :] + pltpu.roll(acc[kw:, :], nhb - 1, 1) + bias
        return jnp.where(lax.broadcasted_iota(jnp.int32, (1, nhb), 1) % period < n_cmp, out, 0.0)

    if k_natural:
        bias = _dot(pe8(0), wflat_ref[0])[0:1, :]
        bias = jnp.concatenate([bias] * NSA_KV_HEADS, axis=1)
        k_out = acc_k[:, :kw] + pltpu.roll(acc_k[:, kw:], nhb - 1, 0) + bias
        k_out = jnp.where(lax.broadcasted_iota(jnp.int32, (nhb, 1), 0) % period < n_cmp, k_out, 0.0)
    else:
        k_out = finish_t(acc_k, 0)
    return k_out, finish_t(acc_v, 1)


def _cmap_t(n_cmp, n_slc, rows, cols):
    c0 = np.arange(cols)[None, :] * CMP_STRIDE
    s0 = np.arange(rows)[:, None] * SLC_BLOCK
    ov = np.minimum(c0 + CMP_LEN, s0 + SLC_BLOCK) - np.maximum(c0, s0)
    ov = np.clip(ov, 0, None).astype(np.float32) / CMP_STRIDE
    keep = (np.arange(cols)[None, :] < n_cmp) & (np.arange(rows)[:, None] < n_slc)
    return np.where(keep, ov, 0.0).astype(np.float32)


def _rank_before(score, n_slc, axis, blk):
    rank = jnp.zeros(score.shape, jnp.int32)
    for j in range(n_slc):
        other = score[j:j + 1, :] if axis == 0 else score[:, j:j + 1]
        before = (other > score) | ((other == score) & (j < blk))
        rank = rank + before.astype(jnp.int32)
    return rank


NSA_QB = 128
NSA_KT = NSA_QB
NSA_HEADS_PER_STEP = 2
NSA_GATE_ROWS = 64
NSA_T_ROWS = Q_WIDTH + 6 * KV_WIDTH + NSA_GATE_ROWS
NSA_NAT_COLS = 4 * KV_WIDTH


def _nsa_proj_kernel(x_ref, wn_ref, wt_ref, bg_ref, *rest, n_prev):
    xc_ref, ks_ref, kw_ref, qt_ref, vst_ref, vwt_ref, gt_ref, kvc_ref, kvs_ref, kvw_ref = rest[n_prev:]
    tq = x_ref.shape[1]
    kw_, hd = KV_WIDTH, HEAD_DIM
    xb = x_ref[0].astype(BF16)
    nat = _dot(xb, wn_ref[...])
    for s in range(CMP_SLABS):
        xc_ref[0, s] = nat[:, s * LANES:(s + 1) * LANES]
    tok = pl.program_id(1) * tq + lax.broadcasted_iota(jnp.int32, (tq, hd), 0)
    onehot = (tok // SLC_BLOCK == lax.broadcasted_iota(jnp.int32, (tq, hd), 1)).astype(BF16)
    for g in range(NSA_KV_HEADS):
        k_slc = nat[:, 2 * kw_ + g * hd:2 * kw_ + (g + 1) * hd].astype(BF16)
        ks_ref[0, g] = jnp.concatenate([k_slc, onehot], axis=1)
        kw_ref[0, g] = nat[:, 3 * kw_ + g * hd:3 * kw_ + (g + 1) * hd].astype(BF16)
    t = _dot_nt(wt_ref[...], xb)
    kv0 = Q_WIDTH
    for g in range(NSA_KV_HEADS):
        r_slc = kv0 + 3 * kw_ + g * hd
        r_win = kv0 + 5 * kw_ + g * hd
        vst_ref[0, g] = t[r_slc:r_slc + hd, :].astype(BF16)
        vwt_ref[0, g] = t[r_win:r_win + hd, :].astype(BF16)
    gates = jax.nn.sigmoid(t[kv0 + 6 * kw_:, :] + bg_ref[...])
    for nb in range(tq // NSA_QB):
        cols = slice(nb * NSA_QB, (nb + 1) * NSA_QB)
        gt_ref[0, nb] = gates[:, cols]
        for br, ref in enumerate((kvc_ref, kvs_ref, kvw_ref)):
            rows = t[kv0 + br * 2 * kw_:kv0 + (br + 1) * 2 * kw_, cols]
            ref[0, nb] = rows.reshape(2, NSA_KV_HEADS, hd, NSA_QB)
        for g in range(NSA_KV_HEADS):
            for r in range(NSA_GROUP):
                r0 = (g * NSA_GROUP + r) * hd
                qt_ref[0, nb, g, :, r * NSA_QB:(r + 1) * NSA_QB] = (
                    t[r0:r0 + hd, cols] * ATTN_SCALE).astype(BF16)


def nsa_prompt_proj(x, w_in, b_gate, tq, layer, n_layers, prev_pages):
    bsz, t, d = x.shape
    assert NSA_QB == PAGE_SIZE and t // SLC_BLOCK <= HEAD_DIM
    nb, nbk = t // NSA_QB, tq // NSA_QB
    g_, hd, w2 = NSA_KV_HEADS, HEAD_DIM, 2 * KV_WIDTH
    w_q = w_in[:, :Q_WIDTH]
    w_kv = w_in[:, Q_WIDTH:Q_WIDTH + 3 * w2]
    w_g = w_in[:, Q_WIDTH + 3 * w2:]
    pad = jnp.zeros((d, NSA_GATE_ROWS - 3 * NSA_HEADS), w_in.dtype)
    w_t = jnp.concatenate([w_q, w_kv, w_g, pad], axis=1).T.astype(BF16)
    w_nat = jnp.concatenate([w_kv[:, :w2], w_kv[:, w2:w2 + KV_WIDTH], w_kv[:, 2 * w2:2 * w2 + KV_WIDTH]],
                            axis=1).astype(BF16)
    bg = jnp.concatenate([b_gate, jnp.zeros((NSA_GATE_ROWS - 3 * NSA_HEADS,), b_gate.dtype)]).reshape(NSA_GATE_ROWS, 1)
    page = jax.ShapeDtypeStruct((n_layers, bsz, nb, 2, g_, hd, NSA_QB), F32)
    page_spec = pl.BlockSpec((None, 1, nbk, 2, g_, hd, NSA_QB), lambda b, i: (layer, b, i, 0, 0, 0, 0))
    out_shape = (
        jax.ShapeDtypeStruct((bsz, CMP_SLABS, t, LANES), F32),
        jax.ShapeDtypeStruct((bsz, g_, t, 2 * hd), BF16), jax.ShapeDtypeStruct((bsz, g_, t, hd), BF16),
        jax.ShapeDtypeStruct((bsz, nb, g_, hd, NSA_GROUP * NSA_QB), BF16),
        jax.ShapeDtypeStruct((bsz, g_, hd, t), BF16), jax.ShapeDtypeStruct((bsz, g_, hd, t), BF16),
        jax.ShapeDtypeStruct((bsz, nb, NSA_GATE_ROWS, NSA_QB), F32),
        page, page, page,
    )
    out_specs = (
        pl.BlockSpec((1, CMP_SLABS, tq, LANES), lambda b, i: (b, 0, i, 0)),
        pl.BlockSpec((1, g_, tq, 2 * hd), lambda b, i: (b, 0, i, 0)),
        pl.BlockSpec((1, g_, tq, hd), lambda b, i: (b, 0, i, 0)),
        pl.BlockSpec((1, nbk, g_, hd, NSA_GROUP * NSA_QB), lambda b, i: (b, i, 0, 0, 0)),
        pl.BlockSpec((1, g_, hd, tq), lambda b, i: (b, 0, 0, i)),
        pl.BlockSpec((1, g_, hd, tq), lambda b, i: (b, 0, 0, i)),
        pl.BlockSpec((1, nbk, NSA_GATE_ROWS, NSA_QB), lambda b, i: (b, i, 0, 0)),
        page_spec, page_spec, page_spec,
    )
    prev = tuple(prev_pages) if prev_pages is not None else ()
    n_out = len(out_shape)
    aliases = {4 + i: n_out - len(prev) + i for i in range(len(prev))}
    return pl.pallas_call(
        functools.partial(_nsa_proj_kernel, n_prev=len(prev)),
        grid=(bsz, t // tq),
        in_specs=[pl.BlockSpec((1, tq, d), lambda b, i: (b, i, 0)),
                  pl.BlockSpec((d, NSA_NAT_COLS), lambda b, i: (0, 0)),
                  pl.BlockSpec((NSA_T_ROWS, d), lambda b, i: (0, 0)),
                  pl.BlockSpec((NSA_GATE_ROWS, 1), lambda b, i: (0, 0))]
                 + [pl.BlockSpec(memory_space=pl.ANY)] * len(prev),
        out_specs=out_specs,
        out_shape=out_shape,
        input_output_aliases=aliases,
        compiler_params=_params("parallel", "parallel"),
        name="nsa_prompt_proj",
    )(x, w_nat, w_t, bg, *prev)


def _nsa_compress_prompt_kernel(x_ref, wk_ref, wv_ref, pe_ref, wflat_ref, wflat_t_ref, kc_ref, vct_ref, *, n_cmp):
    nhb = x_ref.shape[2] // CMP_STRIDE
    load = lambda p, s: x_ref[0, s, pl.ds(p, nhb, stride=CMP_STRIDE), :]
    k_all, v_t = _compress(load, nhb, nhb, n_cmp, wk_ref, wv_ref, pe_ref, wflat_ref, wflat_t_ref, True)
    for g in range(NSA_KV_HEADS):
        kc_ref[0, g] = k_all[:, g * HEAD_DIM:(g + 1) * HEAD_DIM].astype(BF16)
        vct_ref[0, g] = v_t[g * HEAD_DIM:(g + 1) * HEAD_DIM, :].astype(BF16)


def nsa_compress_prompt(kv_cmp, w_cmp, pe_cmp):
    bsz, _, t, _ = kv_cmp.shape
    nhb = t // CMP_STRIDE
    n_cmp = (t - CMP_LEN) // CMP_STRIDE + 1
    ops, specs = _cmp_operands(w_cmp, pe_cmp, True)
    return pl.pallas_call(
        functools.partial(_nsa_compress_prompt_kernel, n_cmp=n_cmp),
        grid=(bsz,),
        in_specs=[pl.BlockSpec((1, CMP_SLABS, t, LANES), lambda b: (b, 0, 0, 0))] + specs,
        out_specs=(pl.BlockSpec((1, NSA_KV_HEADS, nhb, HEAD_DIM), lambda b: (b, 0, 0, 0)),
                   pl.BlockSpec((1, NSA_KV_HEADS, HEAD_DIM, nhb), lambda b: (b, 0, 0, 0))),
        out_shape=(jax.ShapeDtypeStruct((bsz, NSA_KV_HEADS, nhb, HEAD_DIM), BF16),
                   jax.ShapeDtypeStruct((bsz, NSA_KV_HEADS, HEAD_DIM, nhb), BF16)),
        compiler_params=_params("parallel"),
        name="nsa_compress_prompt",
    )(kv_cmp, *ops)


def _attn_bias_consts():
    k = np.arange(NSA_QB)[:, None]
    q = np.arange(NSA_QB)[None, :]
    x4 = lambda a: np.tile(a.astype(np.float32), (1, NSA_GROUP))
    causal = x4(np.where(k <= q, 0.0, NEG_INF))
    lower = x4(np.where(k > q, 0.0, NEG_INF))
    zeros, neg = np.zeros_like(causal), np.full_like(causal, NEG_INF)
    win_bias = np.stack([zeros, causal, neg, lower])
    halves = NSA_KT // NSA_QB
    slc_bias = np.stack([np.concatenate([zeros] * h + [causal] + [neg] * (halves - 1 - h)) for h in range(halves)])
    return win_bias, slc_bias


WB_ZERO, WB_CAUSAL, WB_NEG, WB_LOWER = 0, 1, 2, 3


def _softmax_cols(s, mask):
    sm = jnp.where(mask, s, NEG_INF)
    m = jnp.max(sm, axis=0, keepdims=True)
    e = jnp.where(mask, jnp.exp(sm - m), 0.0)
    l = jnp.sum(e, axis=0, keepdims=True)
    return e / jnp.where(l > 0.0, l, 1.0)


def _lanes_x4(a):
    return jnp.concatenate([a] * NSA_GROUP, axis=1)


SOFTMAX_CHUNK = 128


def _softmax_pv(score_tile, n_tiles, tile, vt):
    n_rows = n_tiles * tile
    sub = 8
    m8 = None
    for j in range(n_tiles):
        s = score_tile(j)
        m_j = jnp.max(s.reshape(tile // sub, sub, s.shape[1]), axis=0)
        m8 = m_j if m8 is None else jnp.maximum(m8, m_j)
    m = jnp.max(m8, axis=0, keepdims=True)
    dh = vt.shape[0]
    vt_ones = jnp.concatenate([vt, jnp.ones((BF16_ROWS, n_rows), BF16)], axis=0)
    acc = None
    for j in range(n_tiles):
        s = score_tile(j)
        for r0 in range(0, tile, SOFTMAX_CHUNK):
            p = jnp.exp(s[r0:r0 + SOFTMAX_CHUNK, :] - m).astype(BF16)
            keys = slice(j * tile + r0, j * tile + r0 + SOFTMAX_CHUNK)
            part = _dot(vt_ones[:, keys], p)
            acc = part if acc is None else acc + part
    return acc[0:dh, :] / acc[dh:dh + 1, :]


def _nsa_prompt_attn_kernel(qt_ref, gt_ref, kc_ref, vct_ref, ks_ref, vst_ref, kw_ref, vwt_ref, cmap_ref,
                            wbias_ref, sbias_ref, o_ref, oslc_ref, *, n_cmp, n_slc, nb):
    qb, kt = NSA_QB, NSA_KT
    width = NSA_GROUP * qb
    n_heads = qt_ref.shape[2]
    g0 = pl.program_id(1) * n_heads
    n = pl.program_id(2)
    s0 = n * qb
    qpos1 = s0 + lax.broadcasted_iota(jnp.int32, (1, qb), 1)
    qpos = _lanes_x4(qpos1)
    ncp = kc_ref.shape[2]
    nsp = cmap_ref.shape[0]

    def compressed_and_select(gi):
        qt = qt_ref[0, 0, gi]
        crow = lax.broadcasted_iota(jnp.int32, (ncp, 1), 0)
        cmask = (crow * CMP_STRIDE + (CMP_LEN - 1) <= qpos) & (crow < n_cmp)
        p_cmp = _softmax_cols(_dot(kc_ref[0, gi], qt), cmask)
        o_cmp = _dot(vct_ref[0, gi], p_cmp.astype(BF16))

        psum = p_cmp[:, 0:qb]
        for r in range(1, NSA_GROUP):
            psum = psum + p_cmp[:, r * qb:(r + 1) * qb]
        imp = sum(_dot(cmap_ref[...], piece) for piece in _split3(psum))
        blk = lax.broadcasted_iota(jnp.int32, (nsp, 1), 0)
        cur = qpos1 // SLC_BLOCK
        forced = (blk == 0) | (blk == cur) | (blk == cur - 1)
        allowed = (blk <= cur) & (blk < n_slc)
        score = jnp.where(allowed, imp + jnp.where(forced, FORCE_BONUS, 0.0), -jnp.inf)
        sel = (_rank_before(score, n_slc, 0, blk) < min(TOP_N, n_slc)) & allowed
        sel_bias = _lanes_x4(jnp.where(sel, 0.0, NEG_INF)).astype(BF16)
        q_ext = jnp.concatenate([qt, sel_bias, jnp.zeros((HEAD_DIM - nsp, width), BF16)], axis=0)
        return qt, o_cmp, q_ext

    heads = [compressed_and_select(gi) for gi in range(n_heads)]

    last = n // (kt // qb)
    for k_last in range(nb // (kt // qb)):
        @pl.when(last == k_last)
        def _():
            for gi in range(n_heads):
                q_ext = heads[gi][2]

                def score_tile(j):
                    s = _dot(ks_ref[0, gi, j * kt:(j + 1) * kt, :], q_ext)
                    return s + sbias_ref[n % (kt // qb)] if j == k_last else s

                oslc_ref[gi] = _softmax_pv(score_tile, k_last + 1, kt, vst_ref[0, gi, :, 0:(k_last + 1) * kt])

    n_win = WINDOW // qb
    n_wt = min(n_win + 1, nb)
    first = jnp.clip(n - (n_wt - 1), 0, nb - n_wt)
    win_rows = pl.ds(pl.multiple_of(first * qb, qb), n_wt * qb)

    for gi in range(n_heads):
        qt, o_cmp, _ = heads[gi]

        def win_tile(i):
            j = first + i
            which = jnp.where(j > n, WB_NEG,
                              jnp.where(j == n, WB_CAUSAL, jnp.where(j == n - n_win, WB_LOWER, WB_ZERO)))
            rows = pl.ds(pl.multiple_of(j * qb, qb), qb)
            return _dot(kw_ref[0, gi, rows, :], qt) + wbias_ref[which]

        o_win = _softmax_pv(win_tile, n_wt, qb, vwt_ref[0, gi, :, win_rows])

        def gate(branch):
            rows = gt_ref[0, 0, pl.ds(branch * NSA_HEADS + (g0 + gi) * NSA_GROUP, NSA_GROUP), :]
            return jnp.concatenate([rows[r:r + 1, :] for r in range(NSA_GROUP)], axis=1)

        o = gate(0) * o_cmp + gate(1) * oslc_ref[gi] + gate(2) * o_win
        for r in range(NSA_GROUP):
            r0 = (gi * NSA_GROUP + r) * HEAD_DIM
            o_ref[0, 0, r0:r0 + HEAD_DIM, :] = o[:, r * qb:(r + 1) * qb]


def nsa_prompt_attention(qt, gt, kc, vct, ks, vst, kw, vwt):
    bsz, nb, g_, hd, width = qt.shape
    t = kw.shape[2]
    assert t % NSA_KT == 0
    ncp = kc.shape[2]
    n_cmp = (t - CMP_LEN) // CMP_STRIDE + 1
    n_slc = -(-t // SLC_BLOCK)
    nsp = -(-n_slc // BF16_ROWS) * BF16_ROWS
    cmap = jnp.asarray(_cmap_t(n_cmp, n_slc, nsp, ncp), BF16)
    win_bias, slc_bias = _attn_bias_consts()
    gs = NSA_HEADS_PER_STEP
    per_bg = lambda *shape: pl.BlockSpec((1, gs) + shape, lambda b, g, n: (b, g, 0, 0))
    const = lambda a: pl.BlockSpec(a.shape, lambda b, g, n: (0,) * a.ndim)
    return pl.pallas_call(
        functools.partial(_nsa_prompt_attn_kernel, n_cmp=n_cmp, n_slc=n_slc, nb=nb),
        grid=(bsz, g_ // gs, nb),
        in_specs=[pl.BlockSpec((1, 1, gs, hd, width), lambda b, g, n: (b, n, g, 0, 0)),
                  pl.BlockSpec((1, 1, NSA_GATE_ROWS, NSA_QB), lambda b, g, n: (b, n, 0, 0)),
                  per_bg(ncp, hd), per_bg(hd, ncp),
                  per_bg(t, 2 * hd), per_bg(hd, t), per_bg(t, hd), per_bg(hd, t),
                  const(cmap), const(win_bias), const(slc_bias)],
        out_specs=pl.BlockSpec((1, 1, gs * NSA_GROUP * hd, NSA_QB), lambda b, g, n: (b, n, g, 0)),
        out_shape=jax.ShapeDtypeStruct((bsz, nb, Q_WIDTH, NSA_QB), F32),
        scratch_shapes=[pltpu.VMEM((gs, hd, width), F32)],
        compiler_params=_params("parallel", "parallel", "arbitrary"),
        name="nsa_prompt_attention",
    )(qt, gt, kc, vct, ks, vst, kw, vwt, cmap, win_bias, slc_bias)


def _proj_t_deepnorm_kernel(at_ref, w_ref, x_ref, g_ref, b_ref, o_ref):
    a = at_ref[0, 0].T.astype(BF16)
    o_ref[0] = _deepnorm_rows(x_ref[0], _dot(a, w_ref[...]), g_ref[...], b_ref[...])


def proj_t_deepnorm(a_t, w_bf, x, g, b):
    bsz, nb, k, qb = a_t.shape
    d = x.shape[2]
    return pl.pallas_call(
        _proj_t_deepnorm_kernel,
        grid=(bsz, nb),
        in_specs=[pl.BlockSpec((1, 1, k, qb), lambda b, n: (b, n, 0, 0)),
                  pl.BlockSpec((k, d), lambda b, n: (0, 0)),
                  pl.BlockSpec((1, qb, d), lambda b, n: (b, n, 0)),
                  pl.BlockSpec((1, d), lambda b, n: (0, 0)),
                  pl.BlockSpec((1, d), lambda b, n: (0, 0))],
        out_specs=pl.BlockSpec((1, qb, d), lambda b, n: (b, n, 0)),
        out_shape=jax.ShapeDtypeStruct(x.shape, F32),
        compiler_params=_params("parallel", "parallel"),
        name="proj_t_deepnorm",
    )(a_t, w_bf, x, g.reshape(1, d), b.reshape(1, d))


def nsa_prompt_layer(x, w_in, b_gate, w_cmp, pe_cmp, w_out, g, b, layer, n_layers, prev_pages):
    xc, ks, kw, qt, vst, vwt, gt, kvc, kvs, kvw = nsa_prompt_proj(x, w_in, b_gate, 512, layer, n_layers, prev_pages)
    kc, vct = nsa_compress_prompt(xc, w_cmp, pe_cmp)
    o_t = nsa_prompt_attention(qt, gt, kc, vct, ks, vst, kw, vwt)
    return proj_t_deepnorm(o_t, w_out.astype(BF16), x, g, b), (kvc, kvs, kvw)


SAMPLE_SEQS_PER_STEP = 2


def _page_specs(cache_t, layer, n_seq, n_pages):
    block = (1, 1) + cache_t.shape[2:]
    return [pl.BlockSpec(block, lambda b, pt, si=si, k=k: (layer, pt[b * n_seq + si, k], 0, 0, 0, 0))
            for si in range(n_seq) for k in range(n_pages)]


def _nsa_compress_sample_kernel(*refs, n_seq, n_pages, n_cmp):
    page_refs = refs[1:1 + n_seq * n_pages]
    wk_ref, wv_ref, pe_ref, wflat_ref, wflat_t_ref, perm_ref, kct_ref, vct_ref, rows_ref = refs[1 + n_seq * n_pages:]
    kw = KV_WIDTH
    hb_page = PAGE_SIZE // CMP_STRIDE
    for k in range(n_seq * n_pages):
        for kv in range(2):
            tile = page_refs[k][0, 0, kv].reshape(kw, PAGE_SIZE).astype(BF16)
            by_p = _dot(tile, perm_ref[...]).T
            for p in range(CMP_STRIDE):
                rows_ref[kv, p, k * hb_page:(k + 1) * hb_page, :] = by_p[p * hb_page:(p + 1) * hb_page, :]
    period = n_pages * hb_page
    slabs_kv = kw // LANES
    load = lambda p, s: rows_ref[s // slabs_kv, p, :, (s % slabs_kv) * LANES:(s % slabs_kv + 1) * LANES]
    k_t, v_t = _compress(load, n_seq * period, period, n_cmp, wk_ref, wv_ref, pe_ref, wflat_ref, wflat_t_ref, False)
    for si in range(n_seq):
        kct_ref[si] = k_t[:, si * period:(si + 1) * period].astype(BF16)
        vct_ref[si] = v_t[:, si * period:(si + 1) * period].astype(BF16)


def nsa_compress_sample(cache_cmp_t, layer, page_table, w_cmp, pe_cmp):
    bsz, n_pages = page_table.shape
    nhb = n_pages * PAGE_SIZE // CMP_STRIDE
    n_cmp = (n_pages * PAGE_SIZE - CMP_LEN) // CMP_STRIDE + 1
    ops, specs = _cmp_operands(w_cmp, pe_cmp, False)
    tok = np.arange(PAGE_SIZE)
    perm = np.zeros((PAGE_SIZE, PAGE_SIZE), np.float32)
    perm[tok, (tok % CMP_STRIDE) * (PAGE_SIZE // CMP_STRIDE) + tok // CMP_STRIDE] = 1.0
    ops = ops + (jnp.asarray(perm, BF16),)
    specs = specs + [pl.BlockSpec(perm.shape, lambda *_: (0, 0))]
    n_seq = SAMPLE_SEQS_PER_STEP
    page_specs = _page_specs(cache_cmp_t, layer, n_seq, n_pages)
    out = jax.ShapeDtypeStruct((bsz, KV_WIDTH, nhb), BF16)
    out_spec = pl.BlockSpec((n_seq, KV_WIDTH, nhb), lambda b, pt: (b, 0, 0))
    return pl.pallas_call(
        functools.partial(_nsa_compress_sample_kernel, n_seq=n_seq, n_pages=n_pages, n_cmp=n_cmp),
        grid_spec=pltpu.PrefetchScalarGridSpec(
            num_scalar_prefetch=1, grid=(bsz // n_seq,),
            in_specs=page_specs + specs, out_specs=(out_spec, out_spec),
            scratch_shapes=[pltpu.VMEM((2, CMP_STRIDE, n_seq * nhb, KV_WIDTH), F32)]),
        out_shape=(out, out),
        compiler_params=_params("parallel"),
        name="nsa_compress_sample",
    )(page_table, *([cache_cmp_t] * (n_seq * n_pages)), *ops)


def _nsa_sample_attn_kernel(pt_ref, q_ref, kvn_ref, gp_ref, bg_ref, kct_ref, vct_ref, win_ref,
                            cmap_ref, g16_ref, e_ref, *rest, past, n_cmp, n_slc):
    del pt_ref
    all_pages, o_ref = rest[:-1], rest[-1]
    n_seq = q_ref.shape[0]
    n_pages = len(all_pages) // n_seq
    for si in range(n_seq):
        _nsa_sample_attn_one(si, q_ref, kvn_ref, gp_ref, bg_ref, kct_ref, vct_ref, win_ref, cmap_ref, g16_ref,
                             e_ref, all_pages[si * n_pages:(si + 1) * n_pages], o_ref, past, n_cmp, n_slc)


def _nsa_sample_attn_one(si, q_ref, kvn_ref, gp_ref, bg_ref, kct_ref, vct_ref, win_ref, cmap_ref, g16_ref,
                         e_ref, page_refs, o_ref, past, n_cmp, n_slc):
    kw = KV_WIDTH
    q16 = q_ref[si] * ATTN_SCALE
    rowg = lax.broadcasted_iota(jnp.int32, (NSA_HEADS, 1), 0) // NSA_GROUP
    colg = lax.broadcasted_iota(jnp.int32, (1, kw), 1) // HEAD_DIM
    own = rowg == colg
    qbd = jnp.where(own, jnp.concatenate([q16] * NSA_KV_HEADS, axis=1), 0.0).astype(BF16)

    def pick(o_full):
        om = jnp.where(own, o_full, 0.0)
        out = om[:, 0:HEAD_DIM]
        for g in range(1, NSA_KV_HEADS):
            out = out + om[:, g * HEAD_DIM:(g + 1) * HEAD_DIM]
        return out

    def new_row(lo):
        return kvn_ref[si][:, lo:lo + kw]

    def attend(k_t, v_t, mask, k_new, v_new, new_ok):
        s = jnp.where(mask, _dot(qbd, k_t), NEG_INF)
        m = jnp.max(s, axis=1, keepdims=True)
        if k_new is not None:
            k8 = jnp.broadcast_to(k_new.astype(BF16), (8, kw))
            s_new = jnp.where(new_ok, _dot_nt(qbd, k8)[:, 0:1], NEG_INF)
            m = jnp.maximum(m, s_new)
        e = jnp.where(mask, jnp.exp(s - m), 0.0)
        l = jnp.sum(e, axis=1, keepdims=True)
        o_full = _dot_nt(e.astype(BF16), v_t)
        if k_new is not None:
            e_new = jnp.where(new_ok, jnp.exp(s_new - m), 0.0)
            l = l + e_new
            o_full = o_full + e_new * v_new
        l = jnp.where(l > 0.0, l, 1.0)
        return pick(o_full) / l, e / l

    q_pos = past
    ncp = kct_ref.shape[2]
    c = lax.broadcasted_iota(jnp.int32, (1, ncp), 1)
    cmask = (c * CMP_STRIDE + (CMP_LEN - 1) <= q_pos) & (c < n_cmp)
    o_cmp, p_cmp = attend(kct_ref[si], vct_ref[si], cmask, None, None, None)

    psum = sum(_dot(g16_ref[...], piece) for piece in _split3(p_cmp))
    imp = sum(_dot(piece, cmap_ref[...]) for piece in _split3(psum))
    nsp = cmap_ref.shape[1]
    blk = lax.broadcasted_iota(jnp.int32, (1, nsp), 1)
    cur = q_pos // SLC_BLOCK
    forced = (blk == 0) | (blk == cur) | (blk == cur - 1)
    allowed = (blk <= cur) & (blk < n_slc)
    score = jnp.where(allowed, imp + jnp.where(forced, FORCE_BONUS, 0.0), -jnp.inf)
    sel = (_rank_before(score, n_slc, 1, blk) < min(TOP_N, n_slc)) & allowed
    sel_keys = _dot(sel.astype(BF16), e_ref[...]) > 0.5

    k_t = jnp.concatenate([r[0, 0, 0].reshape(kw, PAGE_SIZE) for r in page_refs], axis=1).astype(BF16)
    v_t = jnp.concatenate([r[0, 0, 1].reshape(kw, PAGE_SIZE) for r in page_refs], axis=1).astype(BF16)
    new_ok = sel[:, cur:cur + 1]
    o_slc, _ = attend(k_t, v_t, sel_keys, new_row(2 * kw), new_row(3 * kw), new_ok)

    wb = win_ref.shape[-1]
    wpos = past - wb + lax.broadcasted_iota(jnp.int32, (1, wb), 1)
    wmask = (wpos <= q_pos) & (wpos > q_pos - WINDOW) & (wpos >= 0)
    o_win, _ = attend(win_ref[0, si, 0].reshape(kw, wb).astype(BF16), win_ref[0, si, 1].reshape(kw, wb).astype(BF16),
                      wmask, new_row(4 * kw), new_row(5 * kw), True)

    gates = jax.nn.sigmoid(gp_ref[si] + bg_ref[...])
    o_ref[si] = gates[:, 0:1] * o_cmp + gates[:, 1:2] * o_slc + gates[:, 2:3] * o_win


def nsa_sample_layer(x, cache_cmp_t, cache_slc_t, cache_win_t, layer, page_table, w_in, b_gate, w_cmp, pe_cmp,
                     w_out, g, b):
    bsz, d = x.shape
    n_pages = page_table.shape[1]
    past = n_pages * PAGE_SIZE
    w2 = 2 * KV_WIDTH
    n_in = w_in.shape[1]
    n_pad = -(-n_in // LANES) * LANES
    w_pad = jnp.concatenate([w_in, jnp.zeros((d, n_pad - n_in), w_in.dtype)], axis=1).astype(BF16)
    proj = linear(x, w_pad, tm=bsz, tn=n_pad)
    q = proj[:, :Q_WIDTH].reshape(bsz, NSA_HEADS, HEAD_DIM)
    kv_new = proj[:, Q_WIDTH:Q_WIDTH + 3 * w2]
    gate_pre = proj[:, Q_WIDTH + 3 * w2:n_in].reshape(bsz, 3, NSA_HEADS).transpose(0, 2, 1)
    bg = b_gate.reshape(3, NSA_HEADS).T

    kct, vct = nsa_compress_sample(cache_cmp_t, layer, page_table, w_cmp, pe_cmp)
    ncp = kct.shape[2]
    n_cmp = (past - CMP_LEN) // CMP_STRIDE + 1
    assert (past + 1 - CMP_LEN) // CMP_STRIDE + 1 == n_cmp
    n_slc = -(-(past + 1) // SLC_BLOCK)
    nsp = -(-n_slc // LANES) * LANES
    cmap = jnp.asarray(_cmap_t(n_cmp, n_slc, nsp, ncp).T, BF16)
    heads = np.arange(NSA_HEADS)
    g16 = jnp.asarray(heads[:, None] // NSA_GROUP == heads[None, :] // NSA_GROUP, BF16)
    expand = jnp.asarray(np.arange(nsp)[:, None] == np.arange(past)[None, :] // SLC_BLOCK, BF16)

    wb = cache_win_t.shape[-1]
    const = lambda a: pl.BlockSpec(a.shape, lambda b, pt: (0,) * a.ndim)
    n_seq = SAMPLE_SEQS_PER_STEP
    per_b = lambda *shape: pl.BlockSpec((n_seq,) + shape, lambda b, pt: (b,) + (0,) * len(shape))
    page_specs = _page_specs(cache_slc_t, layer, n_seq, n_pages)
    win_spec = pl.BlockSpec((1, n_seq) + cache_win_t.shape[2:], lambda b, pt: (layer, b, 0, 0, 0, 0))
    o = pl.pallas_call(
        functools.partial(_nsa_sample_attn_kernel, past=past, n_cmp=n_cmp, n_slc=n_slc),
        grid_spec=pltpu.PrefetchScalarGridSpec(
            num_scalar_prefetch=1, grid=(bsz // n_seq,),
            in_specs=[per_b(NSA_HEADS, HEAD_DIM), per_b(1, 3 * w2), per_b(NSA_HEADS, 3), const(bg),
                      per_b(KV_WIDTH, ncp), per_b(KV_WIDTH, ncp), win_spec,
                      const(cmap), const(g16), const(expand)] + page_specs,
            out_specs=per_b(NSA_HEADS, HEAD_DIM)),
        out_shape=jax.ShapeDtypeStruct((bsz, NSA_HEADS, HEAD_DIM), F32),
        compiler_params=_params("parallel"),
        name="nsa_sample_attention",
    )(page_table, q, kv_new.reshape(bsz, 1, 3 * w2), gate_pre, bg, kct, vct, cache_win_t, cmap, g16, expand,
      *([cache_slc_t] * (n_seq * n_pages)))
    y = proj_deepnorm(o.reshape(bsz, Q_WIDTH), w_out.astype(BF16), x, g, b, tm=bsz)
    return y, kv_new


HGRN_C = 128
HGRN_SUB = 8
HGRN_LEVELS = (8, 16, 32, 64)


def _hgrn_lower_bound(logits, layer):
    e = jnp.exp(logits - jnp.max(logits, axis=0, keepdims=True))
    sm = e / jnp.sum(e, axis=0, keepdims=True)
    lb = jnp.zeros_like(sm[0])
    for j in range(1, layer + 1):
        lb = lb + sm[j]
    return lb


def _hgrn_gates(f_raw, lb):
    sg = jax.nn.sigmoid(f_raw)
    f = lb + (1.0 - lb) * sg
    log_f = jnp.where(f > 0.0, jnp.log(f), jnp.log1p(-lb) + f_raw)
    return log_f, (1.0 - lb) * (1.0 - sg)


def _rms_gate(o, gain, g_raw):
    o = o * lax.rsqrt(jnp.mean(o * o, axis=-1, keepdims=True) + LN_EPS) * gain
    return o * (g_raw * jax.nn.sigmoid(g_raw))


def _hgrn_prompt_kernel(x_ref, wq_ref, wf_ref, wi_ref, wg_ref, lb_ref, gain_ref, tri_ref, lmask_ref,
                        y_ref, s_ref, st_ref, cum_ref, *, layer):
    c = HGRN_C
    nblk = c // HGRN_SUB
    lb_all = _hgrn_lower_bound(lb_ref[:, 0], layer)
    st_ref[...] = jnp.zeros_like(st_ref)
    rsub = lax.broadcasted_iota(jnp.int32, (c, 1), 0) % HGRN_SUB

    def roll8(a, j):
        return pltpu.roll(a.reshape(nblk, HGRN_SUB, a.shape[1]), j, 1).reshape(a.shape)

    def head_chunk(hi, rows, q, f_raw, v, g_raw):
        lf, k = _hgrn_gates(f_raw, lb_all[:, hi * HGRN_DK:(hi + 1) * HGRN_DK])
        cum = sum(_dot(tri_ref[...], piece) for piece in _split3(lf))
        cum_ref[hi] = cum
        ends = cum_ref[hi, pl.ds(HGRN_SUB - 1, nblk, stride=HGRN_SUB), :]
        end_b = [jnp.broadcast_to(ends[i:i + 1, :], (HGRN_SUB, HGRN_DK)) for i in range(nblk)]
        zero_b = jnp.zeros((HGRN_SUB, HGRN_DK), F32)
        vb = v.astype(BF16)

        a_off = jnp.zeros((c, c), F32)
        for li, lev in enumerate(HGRN_LEVELS):
            per = lev // HGRN_SUB
            before = [end_b[(i // per) * per - 1] if i >= per else zero_b for i in range(nblk)]
            after = [end_b[(i // per) * per + per - 1] for i in range(nblk)]
            qd = q * jnp.exp(cum - jnp.concatenate(before, axis=0))
            kd = k * jnp.exp(jnp.concatenate(after, axis=0) - cum)
            a_off = a_off + _dot_nt(qd.astype(BF16), kd.astype(BF16)) * lmask_ref[li]
        o = _dot(a_off.astype(BF16), vb)

        for j in range(HGRN_SUB):
            if j == 0:
                w = q * k
                vj = v
            else:
                ok = rsub >= j
                w = jnp.where(ok, q * roll8(k, j) * jnp.exp(cum - roll8(cum, j)), 0.0)
                vj = roll8(v, j)
            o = o + jnp.sum(w, axis=1, keepdims=True) * vj

        last = end_b[nblk - 1][0:1, :]
        o = o + _dot_nt((q * jnp.exp(cum)).astype(BF16), st_ref[hi].astype(BF16))
        kd = (k * jnp.exp(last - cum)).astype(BF16)
        st_ref[hi] = st_ref[hi] * jnp.exp(last) + _dot(v.T.astype(BF16), kd)
        y_ref[0, rows, hi * HGRN_DV:(hi + 1) * HGRN_DV] = _rms_gate(o, gain_ref[...], g_raw)

    n_heads = st_ref.shape[0]

    def chunk(ci, _):
        rows = pl.ds(pl.multiple_of(ci * c, c), c)
        xb = x_ref[0, rows, :].astype(BF16)
        q, f_raw, v, g_raw = (_dot(xb, w[...]) for w in (wq_ref, wf_ref, wi_ref, wg_ref))
        for hi in range(n_heads):
            cols = slice(hi * HGRN_DK, (hi + 1) * HGRN_DK)
            head_chunk(hi, rows, q[:, cols], f_raw[:, cols], v[:, cols], g_raw[:, cols])
        return 0

    lax.fori_loop(0, x_ref.shape[1] // c, chunk, 0)
    for hi in range(n_heads):
        s_ref[0, hi] = st_ref[hi].T


HGRN_HEADS_PER_STEP = 4


def hgrn_prompt(x, w_in_bf, lb_logits, gain, layer):
    bsz, t, d = x.shape
    c = HGRN_C
    hh = HGRN_HEADS
    hps = HGRN_HEADS_PER_STEP
    wide = hps * HGRN_DK
    nl = lb_logits.shape[0]
    tri = jnp.asarray(np.tril(np.ones((c, c), np.float32)), BF16)
    ti = np.arange(c)[:, None]
    si = np.arange(c)[None, :]
    lmask = jnp.asarray(np.stack([((ti // (2 * lev) == si // (2 * lev)) & ((ti // lev) % 2 == 1)
                                   & ((si // lev) % 2 == 0)) for lev in HGRN_LEVELS]), F32)
    w_cols = lambda part: pl.BlockSpec((d, wide), lambda b, h: (0, part * (hh // hps) + h))
    return pl.pallas_call(
        functools.partial(_hgrn_prompt_kernel, layer=layer),
        grid=(bsz, hh // hps),
        in_specs=[pl.BlockSpec((1, t, d), lambda b, h: (b, 0, 0)),
                  w_cols(0), w_cols(1), w_cols(2), w_cols(3),
                  pl.BlockSpec((nl, 1, 1, wide), lambda b, h: (0, h, 0, 0)),
                  pl.BlockSpec((1, HGRN_DV), lambda b, h: (0, 0)),
                  pl.BlockSpec((c, c), lambda b, h: (0, 0)),
                  pl.BlockSpec((len(HGRN_LEVELS), c, c), lambda b, h: (0, 0, 0))],
        out_specs=(pl.BlockSpec((1, t, hps * HGRN_DV), lambda b, h: (b, 0, h)),
                   pl.BlockSpec((1, hps, HGRN_DK, HGRN_DV), lambda b, h: (b, h, 0, 0))),
        out_shape=(jax.ShapeDtypeStruct((bsz, t, hh * HGRN_DV), F32),
                   jax.ShapeDtypeStruct((bsz, hh, HGRN_DK, HGRN_DV), F32)),
        scratch_shapes=[pltpu.VMEM((hps, HGRN_DV, HGRN_DK), F32), pltpu.VMEM((hps, c, HGRN_DK), F32)],
        compiler_params=_params("parallel", "arbitrary"),
        name="hgrn_prompt",
    )(x, w_in_bf, w_in_bf, w_in_bf, w_in_bf, lb_logits.reshape(nl, hh // hps, 1, wide),
      gain.reshape(1, HGRN_DV), tri, lmask)


def _hgrn_sample_kernel(p_ref, s0_ref, lb_ref, gain_ref, y_ref, s_ref, *, layer):
    hh = HGRN_HEADS
    n_seq = p_ref.shape[0]
    lb = _hgrn_lower_bound(lb_ref[...], layer)
    parts, rows = [], []
    for si in range(n_seq):
        p = p_ref[si]
        q8, f8, i8, g8 = (p[j * hh:(j + 1) * hh, :] for j in range(4))
        lf8, k8 = _hgrn_gates(f8, lb)
        parts.append((q8, k8, i8, g8))
        rows.append(jnp.exp(lf8))
    pad = [jnp.zeros((LANES - hh * n_seq, HGRN_DK), F32)] if hh * n_seq < LANES else []
    cols = jnp.concatenate(rows + pad, axis=0).T
    zeros = jnp.zeros((BF16_ROWS - 1, HGRN_DK), BF16)
    head_row = lax.broadcasted_iota(jnp.int32, (hh, 1), 0)
    for si in range(n_seq):
        q8, k8, i8, g8 = parts[si]
        q8b = q8.astype(BF16)
        for h in range(hh):
            k16 = jnp.concatenate([k8[h:h + 1, :].astype(BF16), zeros], axis=0)
            i16 = jnp.concatenate([i8[h:h + 1, :].astype(BF16), zeros], axis=0)
            outer = lax.dot_general(k16, i16, (((0,), (0,)), ((), ())), preferred_element_type=F32)
            s_new = cols[:, hh * si + h:hh * si + h + 1] * s0_ref[si, h] + outer
            s_ref[si, h] = s_new
            o_h = _dot(q8b, s_new.astype(BF16))
            o = o_h if h == 0 else jnp.where(head_row == h, o_h, o)
        y_ref[si] = _rms_gate(o, gain_ref[...], g8)


HGRN_SAMPLE_SEQS = 8


def hgrn_sample(proj, s0_all, j, lb_logits, gain, layer):
    bsz = proj.shape[0]
    hh = HGRN_HEADS
    nl = lb_logits.shape[0]
    n_seq = HGRN_SAMPLE_SEQS
    assert bsz % n_seq == 0
    y, s_new = pl.pallas_call(
        functools.partial(_hgrn_sample_kernel, layer=layer),
        grid=(bsz // n_seq,),
        in_specs=[pl.BlockSpec((n_seq, 4 * hh, HGRN_DK), lambda b: (b, 0, 0)),
                  pl.BlockSpec((None, n_seq, hh, HGRN_DK, HGRN_DV), lambda b: (j, b, 0, 0, 0)),
                  pl.BlockSpec((nl, hh, HGRN_DK), lambda b: (0, 0, 0)),
                  pl.BlockSpec((1, HGRN_DV), lambda b: (0, 0))],
        out_specs=(pl.BlockSpec((n_seq, hh, HGRN_DV), lambda b: (b, 0, 0)),
                   pl.BlockSpec((n_seq, hh, HGRN_DK, HGRN_DV), lambda b: (b, 0, 0, 0))),
        out_shape=(jax.ShapeDtypeStruct((bsz, hh, HGRN_DV), F32),
                   jax.ShapeDtypeStruct(s0_all.shape[1:], F32)),
        compiler_params=_params("parallel"),
        name="hgrn_sample",
    )(proj.reshape(bsz, 4 * hh, HGRN_DK), s0_all, lb_logits.reshape(nl, hh, HGRN_DK), gain.reshape(1, HGRN_DV))
    return y.reshape(bsz, hh * HGRN_DV), s_new


TM_PROJ = 512
TM_FFN = 1024
FFN_HC = 256
TM_GMLP = 256
TM_LINEAR = 512
TN_LINEAR = 1024


def kernel(x_prompt, x_sample, cache_cmp_kv, cache_slc_kv, cache_win_kv, state_hgrn, page_table, ln_gain, ln_bias, ffn_w_in, ffn_w_out, nsa_w_in, nsa_b_gate, nsa_w_cmp, nsa_pe_cmp, nsa_w_out, gmlp_w_in, gmlp_b_in, gmlp_ln_v, gmlp_w_sp, gmlp_b_sp, gmlp_w_out, hgrn_w_in, hgrn_lb_logits, hgrn_norm_gain, hgrn_w_out):
    bsz, t, d = x_prompt.shape
    sb = x_sample.shape[0]
    assert x_sample.shape[1] == 1
    xp = x_prompt
    xs = x_sample.reshape(sb, d)
    kv_shape = (2, NSA_KV_HEADS, HEAD_DIM)
    wb_prompt = min(WINDOW, t)
    cmp_t, slc_t, win_t = _token_minor(cache_cmp_kv), _token_minor(cache_slc_kv), _token_minor(cache_win_kv)
    cmp_s, slc_s, win_s, gv_s, hs_p, hs_s = [], [], [], [], [], []
    n_nsa = (DEPTH + N_MIXERS - 1) // N_MIXERS
    kv_pages = None
    ffn_in_bf, ffn_out_bf = ffn_w_in.astype(BF16), ffn_w_out.astype(BF16)
    for layer in range(DEPTH):
        kind, j = layer % N_MIXERS, layer // N_MIXERS
        g0, b0 = ln_gain[layer, 0], ln_bias[layer, 0]
        if kind == 0:
            xp, kv_pages = nsa_prompt_layer(xp, nsa_w_in[j], nsa_b_gate[j], nsa_w_cmp[j], nsa_pe_cmp[j],
                                            nsa_w_out[j], g0, b0, j, n_nsa, kv_pages)
            xs, kv_new = nsa_sample_layer(xs, cmp_t, slc_t, win_t, j, page_table,
                                          nsa_w_in[j], nsa_b_gate[j], nsa_w_cmp[j], nsa_pe_cmp[j], nsa_w_out[j],
                                          g0, b0)
            w2 = 2 * KV_WIDTH
            cmp_s.append(kv_new[:, 0:w2].reshape((sb, 1) + kv_shape))
            slc_s.append(kv_new[:, w2:2 * w2].reshape((sb, 1) + kv_shape))
            win_s.append(kv_new[:, 2 * w2:3 * w2].reshape((sb, 1) + kv_shape))
        elif kind == 1:
            w_in_bf, w_out_bf = gmlp_w_in[j].astype(BF16), gmlp_w_out[j].astype(BF16)
            xp = gmlp_layer(xp.reshape(bsz * t, d), w_in_bf, gmlp_b_in[j], gmlp_ln_v[j], gmlp_w_sp[j], gmlp_b_sp[j],
                            w_out_bf, g0, b0, chunked=True, tm=TM_GMLP).reshape(bsz, t, d)
            xs, v_new = gmlp_layer(xs, w_in_bf, gmlp_b_in[j], gmlp_ln_v[j], gmlp_w_sp[j], gmlp_b_sp[j],
                                   w_out_bf, g0, b0, chunked=False, tm=sb)
            gv_s.append(v_new.reshape(sb, 1, GMLP_WIDTH))
        else:
            w_in_bf, w_out_bf = hgrn_w_in[j].astype(BF16), hgrn_w_out[j].astype(BF16)
            y, s_p = hgrn_prompt(xp, w_in_bf, hgrn_lb_logits, hgrn_norm_gain[j], layer)
            xp = proj_deepnorm(y.reshape(bsz * t, HGRN_WIDTH), w_out_bf, xp.reshape(bsz * t, d), g0, b0,
                               TM_PROJ).reshape(bsz, t, d)
            proj_s = linear(xs, w_in_bf, sb, TN_LINEAR)
            y_s, s_s = hgrn_sample(proj_s, state_hgrn, j, hgrn_lb_logits, hgrn_norm_gain[j], layer)
            xs = proj_deepnorm(y_s, w_out_bf, xs, g0, b0, sb)
            hs_p.append(s_p)
            hs_s.append(s_s)
        g1, b1 = ln_gain[layer, 1], ln_bias[layer, 1]
        xp = ffn_deepnorm(xp.reshape(bsz * t, d), ffn_in_bf, ffn_out_bf, layer, g1, b1, TM_FFN, FFN_HC).reshape(bsz, t, d)
        xs = ffn_deepnorm(xs, ffn_in_bf, ffn_out_bf, layer, g1, b1, sb, FFN_HC)
    kvc, kvs, kvw = kv_pages
    n_page = t // PAGE_SIZE
    pages = lambda a: _token_major(a.reshape((n_nsa, bsz * n_page) + a.shape[3:]))
    win_rows = _token_major(kvw[:, :, n_page - wb_prompt // PAGE_SIZE:])
    win_p = win_rows.reshape((n_nsa, bsz, wb_prompt) + kv_shape)
    return (xp, xs.reshape(sb, 1, d), pages(kvc), jnp.stack(cmp_s), pages(kvs), jnp.stack(slc_s),
            win_p, jnp.stack(win_s), jnp.stack(gv_s), jnp.stack(hs_p), jnp.stack(hs_s))
```

```python
import functools

import jax
import jax.numpy as jnp
import numpy as np
from jax import lax
from jax.experimental import pallas as pl
from jax.experimental.pallas import tpu as pltpu

F32 = jnp.float32
BF16 = jnp.bfloat16

D_MODEL = 1024
DEPTH = 4
PAGE_SIZE = 128
N_MIXERS = 3

NSA_HEADS = 16
NSA_KV_HEADS = 4
NSA_GROUP = NSA_HEADS // NSA_KV_HEADS
HEAD_DIM = D_MODEL // NSA_HEADS
Q_WIDTH = NSA_HEADS * HEAD_DIM
KV_WIDTH = NSA_KV_HEADS * HEAD_DIM
CMP_LEN = 32
CMP_STRIDE = 16
SLC_BLOCK = 64
TOP_N = 16
WINDOW = 512
FORCE_BONUS = 1e3
ATTN_SCALE = HEAD_DIM ** -0.5
NEG_INF = -1e30

GMLP_WIDTH = D_MODEL
GMLP_GROUPS = 8
GMLP_GROUP_DIM = GMLP_WIDTH // GMLP_GROUPS
GMLP_CHUNK = 128

HGRN_HEADS = 8
HGRN_DK = 128
HGRN_DV = 128
HGRN_WIDTH = HGRN_HEADS * HGRN_DK

FFN_HIDDEN = 2816
DEEPNORM_ALPHA = (2 * DEPTH) ** 0.25
LN_EPS = 1e-5

LANES = 128
BF16_ROWS = 16
VMEM_LIMIT = 48 * 1024 * 1024


def _params(*sem):
    return pltpu.CompilerParams(dimension_semantics=sem, vmem_limit_bytes=VMEM_LIMIT)


def _dot(a, b):
    return jnp.dot(a, b, preferred_element_type=F32)


def _dot_nt(a, b):
    return lax.dot_general(a, b, (((1,), (1,)), ((), ())), preferred_element_type=F32)


def _layer_norm_rows(y, g, b):
    mu = jnp.mean(y, axis=-1, keepdims=True)
    yc = y - mu
    var = jnp.mean(yc * yc, axis=-1, keepdims=True)
    return yc * lax.rsqrt(var + LN_EPS) * g + b


def _deepnorm_rows(x, h, g, b):
    return _layer_norm_rows(DEEPNORM_ALPHA * x + h, g, b)


def _split3(x):
    hi = x.astype(BF16)
    r1 = x - hi.astype(F32)
    mid = r1.astype(BF16)
    lo = (r1 - mid.astype(F32)).astype(BF16)
    return hi, mid, lo


def _proj_deepnorm_kernel(a_ref, w_ref, x_ref, g_ref, b_ref, o_ref):
    h = _dot(a_ref[...].astype(BF16), w_ref[...])
    o_ref[...] = _deepnorm_rows(x_ref[...], h, g_ref[...], b_ref[...])


def proj_deepnorm(a, w_bf, x, g, b, tm):
    m, k = a.shape
    d = x.shape[1]
    tm = min(tm, m)
    return pl.pallas_call(
        _proj_deepnorm_kernel,
        grid=(m // tm,),
        in_specs=[pl.BlockSpec((tm, k), lambda i: (i, 0)),
                  pl.BlockSpec((k, d), lambda i: (0, 0)),
                  pl.BlockSpec((tm, d), lambda i: (i, 0)),
                  pl.BlockSpec((1, d), lambda i: (0, 0)),
                  pl.BlockSpec((1, d), lambda i: (0, 0))],
        out_specs=pl.BlockSpec((tm, d), lambda i: (i, 0)),
        out_shape=jax.ShapeDtypeStruct((m, d), F32),
        compiler_params=_params("parallel"),
        name="proj_deepnorm",
    )(a, w_bf, x, g.reshape(1, d), b.reshape(1, d))


def _ffn_kernel(x_ref, wg_ref, wu_ref, wo_ref, g_ref, b_ref, o_ref, xb_ref, acc_ref):
    j = pl.program_id(1)

    @pl.when(j == 0)
    def _():
        xb_ref[...] = x_ref[...].astype(BF16)
        acc_ref[...] = jnp.zeros_like(acc_ref)

    xb = xb_ref[...]
    gate = _dot(xb, wg_ref[...])
    up = _dot(xb, wu_ref[...])
    mid = (gate * jax.nn.sigmoid(gate) * up).astype(BF16)
    acc_ref[...] += _dot(mid, wo_ref[...])

    @pl.when(j == pl.num_programs(1) - 1)
    def _():
        o_ref[...] = _deepnorm_rows(x_ref[...], acc_ref[...], g_ref[...], b_ref[...])


def ffn_deepnorm(x, w_in_bf, w_out_bf, layer, g, b, tm, hc):
    m, d = x.shape
    hidden = w_out_bf.shape[1]
    tm = min(tm, m)
    nh = hidden // hc
    return pl.pallas_call(
        _ffn_kernel,
        grid=(m // tm, nh),
        in_specs=[pl.BlockSpec((tm, d), lambda i, j: (i, 0)),
                  pl.BlockSpec((None, d, hc), lambda i, j: (layer, 0, j)),
                  pl.BlockSpec((None, d, hc), lambda i, j: (layer, 0, nh + j)),
                  pl.BlockSpec((None, hc, d), lambda i, j: (layer, j, 0)),
                  pl.BlockSpec((1, d), lambda i, j: (0, 0)),
                  pl.BlockSpec((1, d), lambda i, j: (0, 0))],
        out_specs=pl.BlockSpec((tm, d), lambda i, j: (i, 0)),
        out_shape=jax.ShapeDtypeStruct((m, d), F32),
        scratch_shapes=[pltpu.VMEM((tm, d), BF16), pltpu.VMEM((tm, d), F32)],
        compiler_params=_params("parallel", "arbitrary"),
        name="ffn_deepnorm",
    )(x, w_in_bf, w_in_bf, w_out_bf, g.reshape(1, d), b.reshape(1, d))


def _gelu_exact(z):
    return 0.5 * z * (1.0 + lax.erf(z * (0.5 ** 0.5)))


def _gmlp_kernel(x_ref, win_ref, bin_ref, lnv_ref, wsp_ref, bsp_ref, wout_ref, g_ref, b_ref,
                 o_ref, *rest, chunked):
    width = GMLP_WIDTH
    x = x_ref[...]
    z = _dot(x.astype(BF16), win_ref[...]) + bin_ref[...]
    gz = _gelu_exact(z)
    u = gz[:, :width]
    v = _layer_norm_rows(gz[:, width:], lnv_ref[0:1, :], lnv_ref[1:2, :])
    if chunked:
        mix_ref, = rest
        vb = v.astype(BF16)
        for c in range(x.shape[0] // GMLP_CHUNK):
            rows = slice(c * GMLP_CHUNK, (c + 1) * GMLP_CHUNK)
            for h in range(GMLP_GROUPS):
                cols = slice(h * GMLP_GROUP_DIM, (h + 1) * GMLP_GROUP_DIM)
                mix_ref[rows, cols] = _dot(wsp_ref[h], vb[rows, cols]) + bsp_ref[:, cols]
        mixed = mix_ref[...]
    else:
        v_ref, = rest
        v_ref[...] = v
        mixed = v * wsp_ref[...] + bsp_ref[...]
    h_out = _dot((u * mixed).astype(BF16), wout_ref[...])
    o_ref[...] = _deepnorm_rows(x, h_out, g_ref[...], b_ref[...])


def gmlp_layer(x, w_in_bf, b_in, ln_v, w_sp, b_sp, w_out_bf, g, b, *, chunked, tm):
    m, d = x.shape
    width = GMLP_WIDTH
    tm = min(tm, m)
    const = lambda *shape: pl.BlockSpec(shape, lambda i: (0,) * len(shape))
    if chunked:
        wsp = jnp.tril(w_sp).astype(BF16)
        bsp = jnp.repeat(b_sp.T, GMLP_GROUP_DIM, axis=1)
        wsp_spec, bsp_spec = const(GMLP_GROUPS, GMLP_CHUNK, GMLP_CHUNK), const(GMLP_CHUNK, width)
        out_shape = jax.ShapeDtypeStruct((m, d), F32)
        out_specs = pl.BlockSpec((tm, d), lambda i: (i, 0))
        scratch = [pltpu.VMEM((tm, width), F32)]
    else:
        wsp = jnp.repeat(w_sp[:, 0, 0], GMLP_GROUP_DIM).reshape(1, width)
        bsp = jnp.repeat(b_sp[:, 0], GMLP_GROUP_DIM).reshape(1, width)
        wsp_spec, bsp_spec = const(1, width), const(1, width)
        out_shape = (jax.ShapeDtypeStruct((m, d), F32), jax.ShapeDtypeStruct((m, width), F32))
        out_specs = (pl.BlockSpec((tm, d), lambda i: (i, 0)), pl.BlockSpec((tm, width), lambda i: (i, 0)))
        scratch = []
    return pl.pallas_call(
        functools.partial(_gmlp_kernel, chunked=chunked),
        grid=(m // tm,),
        in_specs=[pl.BlockSpec((tm, d), lambda i: (i, 0)),
                  const(d, 2 * width), const(1, 2 * width), const(2, width),
                  wsp_spec, bsp_spec, const(width, d), const(1, d), const(1, d)],
        out_specs=out_specs,
        out_shape=out_shape,
        scratch_shapes=scratch,
        compiler_params=_params("parallel"),
        name="gmlp_layer",
    )(x, w_in_bf, b_in.reshape(1, 2 * width), ln_v, wsp, bsp, w_out_bf, g.reshape(1, d), b.reshape(1, d))


def _linear_kernel(x_ref, w_ref, o_ref):
    o_ref[...] = _dot(x_ref[...].astype(BF16), w_ref[...])


def linear(x, w_bf, tm, tn):
    m, k = x.shape
    n = w_bf.shape[1]
    tm, tn = min(tm, m), min(tn, n)
    return pl.pallas_call(
        _linear_kernel,
        grid=(m // tm, n // tn),
        in_specs=[pl.BlockSpec((tm, k), lambda i, j: (i, 0)),
                  pl.BlockSpec((k, tn), lambda i, j: (0, j))],
        out_specs=pl.BlockSpec((tm, tn), lambda i, j: (i, j)),
        out_shape=jax.ShapeDtypeStruct((m, n), F32),
        compiler_params=_params("parallel", "parallel"),
        name="linear",
    )(x, w_bf)


def _token_minor(cache):
    nd = cache.ndim
    return jnp.transpose(cache, tuple(range(nd - 4)) + (nd - 3, nd - 2, nd - 1, nd - 4))


def _token_major(kvt):
    nd = kvt.ndim
    return jnp.transpose(kvt, tuple(range(nd - 4)) + (nd - 1, nd - 4, nd - 3, nd - 2))


CMP_SLABS = 2 * KV_WIDTH // LANES


def _cmp_weights(w_cmp):
    w = w_cmp.reshape(2, 2, CMP_STRIDE, HEAD_DIM, HEAD_DIM)
    eye = jnp.eye(NSA_KV_HEADS, dtype=w_cmp.dtype)
    wbd = jnp.einsum('gh,klpde->pkgdlhe', eye, w)
    return wbd.reshape(CMP_STRIDE, 2, KV_WIDTH, 2 * KV_WIDTH).astype(BF16)


def _cmp_operands(w_cmp, pe_cmp, k_natural):
    wbd = _cmp_weights(w_cmp)
    wk = wbd[:, 0] if k_natural else jnp.swapaxes(wbd[:, 0], 1, 2)
    wv = jnp.swapaxes(wbd[:, 1], 1, 2)
    pe = pe_cmp.reshape(2, CMP_LEN * HEAD_DIM)
    wflat = w_cmp.reshape(2, CMP_LEN * HEAD_DIM, HEAD_DIM).astype(BF16)
    wflat_t = jnp.swapaxes(wflat, 1, 2)
    const = lambda a: pl.BlockSpec(a.shape, lambda *_: (0,) * a.ndim)
    ops = (wk, wv, pe, wflat, wflat_t)
    return ops, [const(a) for a in ops]


def _compress(load_rows, nhb, period, n_cmp, wk_ref, wv_ref, pe_ref, wflat_ref, wflat_t_ref, k_natural):
    kw = KV_WIDTH
    acc_k = jnp.zeros((nhb, 2 * kw) if k_natural else (2 * kw, nhb), F32)
    acc_v = jnp.zeros((2 * kw, nhb), F32)
    for p in range(CMP_STRIDE):
        slabs = [load_rows(p, s).astype(BF16) for s in range(CMP_SLABS)]
        xk = jnp.concatenate(slabs[:CMP_SLABS // 2], axis=1)
        xv = jnp.concatenate(slabs[CMP_SLABS // 2:], axis=1)
        acc_k = acc_k + (_dot(xk, wk_ref[p]) if k_natural else _dot_nt(wk_ref[p], xk))
        acc_v = acc_v + _dot_nt(wv_ref[p], xv)

    def pe8(kv):
        return jnp.broadcast_to(pe_ref[kv:kv + 1, :], (8, CMP_LEN * HEAD_DIM)).astype(BF16)

    def finish_t(acc, kv):
        bias = _dot_nt(wflat_t_ref[kv], pe8(kv))[:, 0:1]
        bias = jnp.concatenate([bias] * NSA_KV_HEADS, axis=0)
        out = acc[:kw, :] + pltpu.roll(acc[kw:, :], nhb - 1, 1) + bias
        return jnp.where(lax.broadcasted_iota(jnp.int32, (1, nhb), 1) % period < n_cmp, out, 0.0)

    if k_natural:
        bias = _dot(pe8(0), wflat_ref[0])[0:1, :]
        bias = jnp.concatenate([bias] * NSA_KV_HEADS, axis=1)
        k_out = acc_k[:, :kw] + pltpu.roll(acc_k[:, kw:], nhb - 1, 0) + bias
        k_out = jnp.where(lax.broadcasted_iota(jnp.int32, (nhb, 1), 0) % period < n_cmp, k_out, 0.0)
    else:
        k_out = finish_t(acc_k, 0)
    return k_out, finish_t(acc_v, 1)


def _cmap_t(n_cmp, n_slc, rows, cols):
    c0 = np.arange(cols)[None, :] * CMP_STRIDE
    s0 = np.arange(rows)[:, None] * SLC_BLOCK
    ov = np.minimum(c0 + CMP_LEN, s0 + SLC_BLOCK) - np.maximum(c0, s0)
    ov = np.clip(ov, 0, None).astype(np.float32) / CMP_STRIDE
    keep = (np.arange(cols)[None, :] < n_cmp) & (np.arange(rows)[:, None] < n_slc)
    return np.where(keep, ov, 0.0).astype(np.float32)


def _rank_before(score, n_slc, axis, blk):
    rank = jnp.zeros(score.shape, jnp.int32)
    for j in range(n_slc):
        other = score[j:j + 1, :] if axis == 0 else score[:, j:j + 1]
        before = (other > score) | ((other == score) & (j < blk))
        rank = rank + before.astype(jnp.int32)
    return rank


NSA_QB = 128
NSA_KT = 2 * NSA_QB
NSA_HEADS_PER_STEP = 2
NSA_GATE_ROWS = 64
NSA_T_ROWS = Q_WIDTH + 6 * KV_WIDTH + NSA_GATE_ROWS
NSA_NAT_COLS = 4 * KV_WIDTH


def _nsa_proj_kernel(x_ref, wn_ref, wt_ref, bg_ref, *rest, n_prev):
    xc_ref, ks_ref, kw_ref, qt_ref, vst_ref, vwt_ref, gt_ref, kvc_ref, kvs_ref, kvw_ref = rest[n_prev:]
    tq = x_ref.shape[1]
    kw_, hd = KV_WIDTH, HEAD_DIM
    xb = x_ref[0].astype(BF16)
    nat = _dot(xb, wn_ref[...])
    for s in range(CMP_SLABS):
        xc_ref[0, s] = nat[:, s * LANES:(s + 1) * LANES]
    tok = pl.program_id(1) * tq + lax.broadcasted_iota(jnp.int32, (tq, hd), 0)
    onehot = (tok // SLC_BLOCK == lax.broadcasted_iota(jnp.int32, (tq, hd), 1)).astype(BF16)
    for g in range(NSA_KV_HEADS):
        k_slc = nat[:, 2 * kw_ + g * hd:2 * kw_ + (g + 1) * hd].astype(BF16)
        ks_ref[0, g] = jnp.concatenate([k_slc, onehot], axis=1)
        kw_ref[0, g] = nat[:, 3 * kw_ + g * hd:3 * kw_ + (g + 1) * hd].astype(BF16)
    t = _dot_nt(wt_ref[...], xb)
    kv0 = Q_WIDTH
    for g in range(NSA_KV_HEADS):
        r_slc = kv0 + 3 * kw_ + g * hd
        r_win = kv0 + 5 * kw_ + g * hd
        vst_ref[0, g] = t[r_slc:r_slc + hd, :].astype(BF16)
        vwt_ref[0, g] = t[r_win:r_win + hd, :].astype(BF16)
    gates = jax.nn.sigmoid(t[kv0 + 6 * kw_:, :] + bg_ref[...])
    for nb in range(tq // NSA_QB):
        cols = slice(nb * NSA_QB, (nb + 1) * NSA_QB)
        gt_ref[0, nb] = gates[:, cols]
        for br, ref in enumerate((kvc_ref, kvs_ref, kvw_ref)):
            rows = t[kv0 + br * 2 * kw_:kv0 + (br + 1) * 2 * kw_, cols]
            ref[0, nb] = rows.reshape(2, NSA_KV_HEADS, hd, NSA_QB)
        for g in range(NSA_KV_HEADS):
            for r in range(NSA_GROUP):
                r0 = (g * NSA_GROUP + r) * hd
                qt_ref[0, nb, g, :, r * NSA_QB:(r + 1) * NSA_QB] = (
                    t[r0:r0 + hd, cols] * ATTN_SCALE).astype(BF16)


def nsa_prompt_proj(x, w_in, b_gate, tq, layer, n_layers, prev_pages):
    bsz, t, d = x.shape
    assert NSA_QB == PAGE_SIZE and t // SLC_BLOCK <= HEAD_DIM
    nb, nbk = t // NSA_QB, tq // NSA_QB
    g_, hd, w2 = NSA_KV_HEADS, HEAD_DIM, 2 * KV_WIDTH
    w_q = w_in[:, :Q_WIDTH]
    w_kv = w_in[:, Q_WIDTH:Q_WIDTH + 3 * w2]
    w_g = w_in[:, Q_WIDTH + 3 * w2:]
    pad = jnp.zeros((d, NSA_GATE_ROWS - 3 * NSA_HEADS), w_in.dtype)
    w_t = jnp.concatenate([w_q, w_kv, w_g, pad], axis=1).T.astype(BF16)
    w_nat = jnp.concatenate([w_kv[:, :w2], w_kv[:, w2:w2 + KV_WIDTH], w_kv[:, 2 * w2:2 * w2 + KV_WIDTH]],
                            axis=1).astype(BF16)
    bg = jnp.concatenate([b_gate, jnp.zeros((NSA_GATE_ROWS - 3 * NSA_HEADS,), b_gate.dtype)]).reshape(NSA_GATE_ROWS, 1)
    page = jax.ShapeDtypeStruct((n_layers, bsz, nb, 2, g_, hd, NSA_QB), F32)
    page_spec = pl.BlockSpec((None, 1, nbk, 2, g_, hd, NSA_QB), lambda b, i: (layer, b, i, 0, 0, 0, 0))
    out_shape = (
        jax.ShapeDtypeStruct((bsz, CMP_SLABS, t, LANES), F32),
        jax.ShapeDtypeStruct((bsz, g_, t, 2 * hd), BF16), jax.ShapeDtypeStruct((bsz, g_, t, hd), BF16),
        jax.ShapeDtypeStruct((bsz, nb, g_, hd, NSA_GROUP * NSA_QB), BF16),
        jax.ShapeDtypeStruct((bsz, g_, hd, t), BF16), jax.ShapeDtypeStruct((bsz, g_, hd, t), BF16),
        jax.ShapeDtypeStruct((bsz, nb, NSA_GATE_ROWS, NSA_QB), F32),
        page, page, page,
    )
    out_specs = (
        pl.BlockSpec((1, CMP_SLABS, tq, LANES), lambda b, i: (b, 0, i, 0)),
        pl.BlockSpec((1, g_, tq, 2 * hd), lambda b, i: (b, 0, i, 0)),
        pl.BlockSpec((1, g_, tq, hd), lambda b, i: (b, 0, i, 0)),
        pl.BlockSpec((1, nbk, g_, hd, NSA_GROUP * NSA_QB), lambda b, i: (b, i, 0, 0, 0)),
        pl.BlockSpec((1, g_, hd, tq), lambda b, i: (b, 0, 0, i)),
        pl.BlockSpec((1, g_, hd, tq), lambda b, i: (b, 0, 0, i)),
        pl.BlockSpec((1, nbk, NSA_GATE_ROWS, NSA_QB), lambda b, i: (b, i, 0, 0)),
        page_spec, page_spec, page_spec,
    )
    prev = tuple(prev_pages) if prev_pages is not None else ()
    n_out = len(out_shape)
    aliases = {4 + i: n_out - len(prev) + i for i in range(len(prev))}
    return pl.pallas_call(
        functools.partial(_nsa_proj_kernel, n_prev=len(prev)),
        grid=(bsz, t // tq),
        in_specs=[pl.BlockSpec((1, tq, d), lambda b, i: (b, i, 0)),
                  pl.BlockSpec((d, NSA_NAT_COLS), lambda b, i: (0, 0)),
                  pl.BlockSpec((NSA_T_ROWS, d), lambda b, i: (0, 0)),
                  pl.BlockSpec((NSA_GATE_ROWS, 1), lambda b, i: (0, 0))]
                 + [pl.BlockSpec(memory_space=pl.ANY)] * len(prev),
        out_specs=out_specs,
        out_shape=out_shape,
        input_output_aliases=aliases,
        compiler_params=_params("parallel", "parallel"),
        name="nsa_prompt_proj",
    )(x, w_nat, w_t, bg, *prev)


def _nsa_compress_prompt_kernel(x_ref, wk_ref, wv_ref, pe_ref, wflat_ref, wflat_t_ref, kc_ref, vct_ref, *, n_cmp):
    nhb = x_ref.shape[2] // CMP_STRIDE
    load = lambda p, s: x_ref[0, s, pl.ds(p, nhb, stride=CMP_STRIDE), :]
    k_all, v_t = _compress(load, nhb, nhb, n_cmp, wk_ref, wv_ref, pe_ref, wflat_ref, wflat_t_ref, True)
    for g in range(NSA_KV_HEADS):
        kc_ref[0, g] = k_all[:, g * HEAD_DIM:(g + 1) * HEAD_DIM].astype(BF16)
        vct_ref[0, g] = v_t[g * HEAD_DIM:(g + 1) * HEAD_DIM, :].astype(BF16)


def nsa_compress_prompt(kv_cmp, w_cmp, pe_cmp):
    bsz, _, t, _ = kv_cmp.shape
    nhb = t // CMP_STRIDE
    n_cmp = (t - CMP_LEN) // CMP_STRIDE + 1
    ops, specs = _cmp_operands(w_cmp, pe_cmp, True)
    return pl.pallas_call(
        functools.partial(_nsa_compress_prompt_kernel, n_cmp=n_cmp),
        grid=(bsz,),
        in_specs=[pl.BlockSpec((1, CMP_SLABS, t, LANES), lambda b: (b, 0, 0, 0))] + specs,
        out_specs=(pl.BlockSpec((1, NSA_KV_HEADS, nhb, HEAD_DIM), lambda b: (b, 0, 0, 0)),
                   pl.BlockSpec((1, NSA_KV_HEADS, HEAD_DIM, nhb), lambda b: (b, 0, 0, 0))),
        out_shape=(jax.ShapeDtypeStruct((bsz, NSA_KV_HEADS, nhb, HEAD_DIM), BF16),
                   jax.ShapeDtypeStruct((bsz, NSA_KV_HEADS, HEAD_DIM, nhb), BF16)),
        compiler_params=_params("parallel"),
        name="nsa_compress_prompt",
    )(kv_cmp, *ops)


def _attn_bias_consts():
    k = np.arange(NSA_QB)[:, None]
    q = np.arange(NSA_QB)[None, :]
    x4 = lambda a: np.tile(a.astype(np.float32), (1, NSA_GROUP))
    causal = x4(np.where(k <= q, 0.0, NEG_INF))
    lower = x4(np.where(k > q, 0.0, NEG_INF))
    zeros, neg = np.zeros_like(causal), np.full_like(causal, NEG_INF)
    win_bias = np.stack([zeros, causal, neg, lower])
    halves = NSA_KT // NSA_QB
    slc_bias = np.stack([np.concatenate([zeros] * h + [causal] + [neg] * (halves - 1 - h)) for h in range(halves)])
    return win_bias, slc_bias


WB_ZERO, WB_CAUSAL, WB_NEG, WB_LOWER = 0, 1, 2, 3


def _softmax_cols(s, mask):
    sm = jnp.where(mask, s, NEG_INF)
    m = jnp.max(sm, axis=0, keepdims=True)
    e = jnp.where(mask, jnp.exp(sm - m), 0.0)
    l = jnp.sum(e, axis=0, keepdims=True)
    return e / jnp.where(l > 0.0, l, 1.0)


def _lanes_x4(a):
    return jnp.concatenate([a] * NSA_GROUP, axis=1)


SOFTMAX_CHUNK = 128


def _softmax_pv(score_tile, n_tiles, tile, s_ref, p_ref, vt):
    n_rows = n_tiles * tile
    width = s_ref.shape[1]
    sub = 8
    by_vreg = lambda a: a.reshape(a.shape[0] // sub, sub, width)
    m8 = None
    for j in range(n_tiles):
        s = score_tile(j)
        s_ref[j * tile:(j + 1) * tile, :] = s
        m_j = jnp.max(by_vreg(s), axis=0)
        m8 = m_j if m8 is None else jnp.maximum(m8, m_j)
    m = jnp.max(m8, axis=0, keepdims=True)
    l8 = jnp.zeros((sub, width), F32)
    for r0 in range(0, n_rows, SOFTMAX_CHUNK):
        rows = slice(r0, r0 + SOFTMAX_CHUNK)
        e = jnp.exp(s_ref[rows, :] - m)
        l8 = l8 + jnp.sum(by_vreg(e), axis=0)
        p_ref[rows, :] = e.astype(BF16)
    l = jnp.sum(l8, axis=0, keepdims=True)
    return _dot(vt, p_ref[0:n_rows, :]) / l


def _nsa_prompt_attn_kernel(qt_ref, gt_ref, kc_ref, vct_ref, ks_ref, vst_ref, kw_ref, vwt_ref, cmap_ref,
                            wbias_ref, sbias_ref, o_ref, s_ref, p_ref, oslc_ref, *, n_cmp, n_slc, nb):
    qb, kt = NSA_QB, NSA_KT
    width = NSA_GROUP * qb
    n_heads = qt_ref.shape[2]
    g0 = pl.program_id(1) * n_heads
    n = pl.program_id(2)
    s0 = n * qb
    qpos1 = s0 + lax.broadcasted_iota(jnp.int32, (1, qb), 1)
    qpos = _lanes_x4(qpos1)
    ncp = kc_ref.shape[2]
    nsp = cmap_ref.shape[0]

    def compressed_and_select(gi):
        qt = qt_ref[0, 0, gi]
        crow = lax.broadcasted_iota(jnp.int32, (ncp, 1), 0)
        cmask = (crow * CMP_STRIDE + (CMP_LEN - 1) <= qpos) & (crow < n_cmp)
        p_cmp = _softmax_cols(_dot(kc_ref[0, gi], qt), cmask)
        o_cmp = _dot(vct_ref[0, gi], p_cmp.astype(BF16))

        psum = p_cmp[:, 0:qb]
        for r in range(1, NSA_GROUP):
            psum = psum + p_cmp[:, r * qb:(r + 1) * qb]
        imp = sum(_dot(cmap_ref[...], piece) for piece in _split3(psum))
        blk = lax.broadcasted_iota(jnp.int32, (nsp, 1), 0)
        cur = qpos1 // SLC_BLOCK
        forced = (blk == 0) | (blk == cur) | (blk == cur - 1)
        allowed = (blk <= cur) & (blk < n_slc)
        score = jnp.where(allowed, imp + jnp.where(forced, FORCE_BONUS, 0.0), -jnp.inf)
        sel = (_rank_before(score, n_slc, 0, blk) < min(TOP_N, n_slc)) & allowed
        sel_bias = _lanes_x4(jnp.where(sel, 0.0, NEG_INF)).astype(BF16)
        q_ext = jnp.concatenate([qt, sel_bias, jnp.zeros((HEAD_DIM - nsp, width), BF16)], axis=0)
        return qt, o_cmp, q_ext

    heads = [compressed_and_select(gi) for gi in range(n_heads)]

    last = n // (kt // qb)
    for k_last in range(nb // (kt // qb)):
        @pl.when(last == k_last)
        def _():
            for gi in range(n_heads):
                q_ext = heads[gi][2]

                def score_tile(j):
                    s = _dot(ks_ref[0, gi, j * kt:(j + 1) * kt, :], q_ext)
                    return s + sbias_ref[n % (kt // qb)] if j == k_last else s

                oslc_ref[gi] = _softmax_pv(score_tile, k_last + 1, kt, s_ref.at[gi], p_ref.at[gi],
                                           vst_ref[0, gi, :, 0:(k_last + 1) * kt])

    n_win = WINDOW // qb
    n_wt = min(n_win + 1, nb)
    first = jnp.clip(n - (n_wt - 1), 0, nb - n_wt)
    win_rows = pl.ds(pl.multiple_of(first * qb, qb), n_wt * qb)

    for gi in range(n_heads):
        qt, o_cmp, _ = heads[gi]

        def win_tile(i):
            j = first + i
            which = jnp.where(j > n, WB_NEG,
                              jnp.where(j == n, WB_CAUSAL, jnp.where(j == n - n_win, WB_LOWER, WB_ZERO)))
            rows = pl.ds(pl.multiple_of(j * qb, qb), qb)
            return _dot(kw_ref[0, gi, rows, :], qt) + wbias_ref[which]

        o_win = _softmax_pv(win_tile, n_wt, qb, s_ref.at[gi], p_ref.at[gi], vwt_ref[0, gi, :, win_rows])

        def gate(branch):
            rows = gt_ref[0, 0, pl.ds(branch * NSA_HEADS + (g0 + gi) * NSA_GROUP, NSA_GROUP), :]
            return jnp.concatenate([rows[r:r + 1, :] for r in range(NSA_GROUP)], axis=1)

        o = gate(0) * o_cmp + gate(1) * oslc_ref[gi] + gate(2) * o_win
        for r in range(NSA_GROUP):
            r0 = (gi * NSA_GROUP + r) * HEAD_DIM
            o_ref[0, 0, r0:r0 + HEAD_DIM, :] = o[:, r * qb:(r + 1) * qb]


def nsa_prompt_attention(qt, gt, kc, vct, ks, vst, kw, vwt):
    bsz, nb, g_, hd, width = qt.shape
    t = kw.shape[2]
    assert t % NSA_KT == 0
    ncp = kc.shape[2]
    n_cmp = (t - CMP_LEN) // CMP_STRIDE + 1
    n_slc = -(-t // SLC_BLOCK)
    nsp = -(-n_slc // BF16_ROWS) * BF16_ROWS
    cmap = jnp.asarray(_cmap_t(n_cmp, n_slc, nsp, ncp), BF16)
    win_bias, slc_bias = _attn_bias_consts()
    gs = NSA_HEADS_PER_STEP
    per_bg = lambda *shape: pl.BlockSpec((1, gs) + shape, lambda b, g, n: (b, g, 0, 0))
    const = lambda a: pl.BlockSpec(a.shape, lambda b, g, n: (0,) * a.ndim)
    return pl.pallas_call(
        functools.partial(_nsa_prompt_attn_kernel, n_cmp=n_cmp, n_slc=n_slc, nb=nb),
        grid=(bsz, g_ // gs, nb),
        in_specs=[pl.BlockSpec((1, 1, gs, hd, width), lambda b, g, n: (b, n, g, 0, 0)),
                  pl.BlockSpec((1, 1, NSA_GATE_ROWS, NSA_QB), lambda b, g, n: (b, n, 0, 0)),
                  per_bg(ncp, hd), per_bg(hd, ncp),
                  per_bg(t, 2 * hd), per_bg(hd, t), per_bg(t, hd), per_bg(hd, t),
                  const(cmap), const(win_bias), const(slc_bias)],
        out_specs=pl.BlockSpec((1, 1, gs * NSA_GROUP * hd, NSA_QB), lambda b, g, n: (b, n, g, 0)),
        out_shape=jax.ShapeDtypeStruct((bsz, nb, Q_WIDTH, NSA_QB), F32),
        scratch_shapes=[pltpu.VMEM((gs, t, width), F32), pltpu.VMEM((gs, t, width), BF16),
                        pltpu.VMEM((gs, hd, width), F32)],
        compiler_params=_params("parallel", "parallel", "arbitrary"),
        name="nsa_prompt_attention",
    )(qt, gt, kc, vct, ks, vst, kw, vwt, cmap, win_bias, slc_bias)


def _proj_t_deepnorm_kernel(at_ref, w_ref, x_ref, g_ref, b_ref, o_ref):
    a = at_ref[0, 0].T.astype(BF16)
    o_ref[0] = _deepnorm_rows(x_ref[0], _dot(a, w_ref[...]), g_ref[...], b_ref[...])


def proj_t_deepnorm(a_t, w_bf, x, g, b):
    bsz, nb, k, qb = a_t.shape
    d = x.shape[2]
    return pl.pallas_call(
        _proj_t_deepnorm_kernel,
        grid=(bsz, nb),
        in_specs=[pl.BlockSpec((1, 1, k, qb), lambda b, n: (b, n, 0, 0)),
                  pl.BlockSpec((k, d), lambda b, n: (0, 0)),
                  pl.BlockSpec((1, qb, d), lambda b, n: (b, n, 0)),
                  pl.BlockSpec((1, d), lambda b, n: (0, 0)),
                  pl.BlockSpec((1, d), lambda b, n: (0, 0))],
        out_specs=pl.BlockSpec((1, qb, d), lambda b, n: (b, n, 0)),
        out_shape=jax.ShapeDtypeStruct(x.shape, F32),
        compiler_params=_params("parallel", "parallel"),
        name="proj_t_deepnorm",
    )(a_t, w_bf, x, g.reshape(1, d), b.reshape(1, d))


def nsa_prompt_layer(x, w_in, b_gate, w_cmp, pe_cmp, w_out, g, b, layer, n_layers, prev_pages):
    xc, ks, kw, qt, vst, vwt, gt, kvc, kvs, kvw = nsa_prompt_proj(x, w_in, b_gate, 512, layer, n_layers, prev_pages)
    kc, vct = nsa_compress_prompt(xc, w_cmp, pe_cmp)
    o_t = nsa_prompt_attention(qt, gt, kc, vct, ks, vst, kw, vwt)
    return proj_t_deepnorm(o_t, w_out.astype(BF16), x, g, b), (kvc, kvs, kvw)


SAMPLE_SEQS_PER_STEP = 2


def _page_specs(cache_t, layer, n_seq, n_pages):
    block = (1, 1) + cache_t.shape[2:]
    return [pl.BlockSpec(block, lambda b, pt, si=si, k=k: (layer, pt[b * n_seq + si, k], 0, 0, 0, 0))
            for si in range(n_seq) for k in range(n_pages)]


def _nsa_compress_sample_kernel(*refs, n_seq, n_pages, n_cmp):
    page_refs = refs[1:1 + n_seq * n_pages]
    wk_ref, wv_ref, pe_ref, wflat_ref, wflat_t_ref, perm_ref, kct_ref, vct_ref, rows_ref = refs[1 + n_seq * n_pages:]
    kw = KV_WIDTH
    hb_page = PAGE_SIZE // CMP_STRIDE
    for k in range(n_seq * n_pages):
        for kv in range(2):
            tile = page_refs[k][0, 0, kv].reshape(kw, PAGE_SIZE).astype(BF16)
            by_p = _dot(tile, perm_ref[...]).T
            for p in range(CMP_STRIDE):
                rows_ref[kv, p, k * hb_page:(k + 1) * hb_page, :] = by_p[p * hb_page:(p + 1) * hb_page, :]
    period = n_pages * hb_page
    slabs_kv = kw // LANES
    load = lambda p, s: rows_ref[s // slabs_kv, p, :, (s % slabs_kv) * LANES:(s % slabs_kv + 1) * LANES]
    k_t, v_t = _compress(load, n_seq * period, period, n_cmp, wk_ref, wv_ref, pe_ref, wflat_ref, wflat_t_ref, False)
    for si in range(n_seq):
        kct_ref[si] = k_t[:, si * period:(si + 1) * period].astype(BF16)
        vct_ref[si] = v_t[:, si * period:(si + 1) * period].astype(BF16)


def nsa_compress_sample(cache_cmp_t, layer, page_table, w_cmp, pe_cmp):
    bsz, n_pages = page_table.shape
    nhb = n_pages * PAGE_SIZE // CMP_STRIDE
    n_cmp = (n_pages * PAGE_SIZE - CMP_LEN) // CMP_STRIDE + 1
    ops, specs = _cmp_operands(w_cmp, pe_cmp, False)
    tok = np.arange(PAGE_SIZE)
    perm = np.zeros((PAGE_SIZE, PAGE_SIZE), np.float32)
    perm[tok, (tok % CMP_STRIDE) * (PAGE_SIZE // CMP_STRIDE) + tok // CMP_STRIDE] = 1.0
    ops = ops + (jnp.asarray(perm, BF16),)
    specs = specs + [pl.BlockSpec(perm.shape, lambda *_: (0, 0))]
    n_seq = SAMPLE_SEQS_PER_STEP
    page_specs = _page_specs(cache_cmp_t, layer, n_seq, n_pages)
    out = jax.ShapeDtypeStruct((bsz, KV_WIDTH, nhb), BF16)
    out_spec = pl.BlockSpec((n_seq, KV_WIDTH, nhb), lambda b, pt: (b, 0, 0))
    return pl.pallas_call(
        functools.partial(_nsa_compress_sample_kernel, n_seq=n_seq, n_pages=n_pages, n_cmp=n_cmp),
        grid_spec=pltpu.PrefetchScalarGridSpec(
            num_scalar_prefetch=1, grid=(bsz // n_seq,),
            in_specs=page_specs + specs, out_specs=(out_spec, out_spec),
            scratch_shapes=[pltpu.VMEM((2, CMP_STRIDE, n_seq * nhb, KV_WIDTH), F32)]),
        out_shape=(out, out),
        compiler_params=_params("parallel"),
        name="nsa_compress_sample",
    )(page_table, *([cache_cmp_t] * (n_seq * n_pages)), *ops)


def _nsa_sample_attn_kernel(pt_ref, q_ref, kvn_ref, gp_ref, bg_ref, kct_ref, vct_ref, win_ref,
                            cmap_ref, g16_ref, e_ref, *rest, past, n_cmp, n_slc):
    del pt_ref
    all_pages, o_ref = rest[:-1], rest[-1]
    n_seq = q_ref.shape[0]
    n_pages = len(all_pages) // n_seq
    for si in range(n_seq):
        _nsa_sample_attn_one(si, q_ref, kvn_ref, gp_ref, bg_ref, kct_ref, vct_ref, win_ref, cmap_ref, g16_ref,
                             e_ref, all_pages[si * n_pages:(si + 1) * n_pages], o_ref, past, n_cmp, n_slc)


def _nsa_sample_attn_one(si, q_ref, kvn_ref, gp_ref, bg_ref, kct_ref, vct_ref, win_ref, cmap_ref, g16_ref,
                         e_ref, page_refs, o_ref, past, n_cmp, n_slc):
    kw = KV_WIDTH
    q16 = q_ref[si] * ATTN_SCALE
    rowg = lax.broadcasted_iota(jnp.int32, (NSA_HEADS, 1), 0) // NSA_GROUP
    colg = lax.broadcasted_iota(jnp.int32, (1, kw), 1) // HEAD_DIM
    own = rowg == colg
    qbd = jnp.where(own, jnp.concatenate([q16] * NSA_KV_HEADS, axis=1), 0.0).astype(BF16)

    def pick(o_full):
        om = jnp.where(own, o_full, 0.0)
        out = om[:, 0:HEAD_DIM]
        for g in range(1, NSA_KV_HEADS):
            out = out + om[:, g * HEAD_DIM:(g + 1) * HEAD_DIM]
        return out

    def new_row(lo):
        return kvn_ref[si][:, lo:lo + kw]

    def attend(k_t, v_t, mask, k_new, v_new, new_ok):
        s = jnp.where(mask, _dot(qbd, k_t), NEG_INF)
        m = jnp.max(s, axis=1, keepdims=True)
        if k_new is not None:
            k8 = jnp.broadcast_to(k_new.astype(BF16), (8, kw))
            s_new = jnp.where(new_ok, _dot_nt(qbd, k8)[:, 0:1], NEG_INF)
            m = jnp.maximum(m, s_new)
        e = jnp.where(mask, jnp.exp(s - m), 0.0)
        l = jnp.sum(e, axis=1, keepdims=True)
        o_full = _dot_nt(e.astype(BF16), v_t)
        if k_new is not None:
            e_new = jnp.where(new_ok, jnp.exp(s_new - m), 0.0)
            l = l + e_new
            o_full = o_full + e_new * v_new
        l = jnp.where(l > 0.0, l, 1.0)
        return pick(o_full) / l, e / l

    q_pos = past
    ncp = kct_ref.shape[2]
    c = lax.broadcasted_iota(jnp.int32, (1, ncp), 1)
    cmask = (c * CMP_STRIDE + (CMP_LEN - 1) <= q_pos) & (c < n_cmp)
    o_cmp, p_cmp = attend(kct_ref[si], vct_ref[si], cmask, None, None, None)

    psum = sum(_dot(g16_ref[...], piece) for piece in _split3(p_cmp))
    imp = sum(_dot(piece, cmap_ref[...]) for piece in _split3(psum))
    nsp = cmap_ref.shape[1]
    blk = lax.broadcasted_iota(jnp.int32, (1, nsp), 1)
    cur = q_pos // SLC_BLOCK
    forced = (blk == 0) | (blk == cur) | (blk == cur - 1)
    allowed = (blk <= cur) & (blk < n_slc)
    score = jnp.where(allowed, imp + jnp.where(forced, FORCE_BONUS, 0.0), -jnp.inf)
    sel = (_rank_before(score, n_slc, 1, blk) < min(TOP_N, n_slc)) & allowed
    sel_keys = _dot(sel.astype(BF16), e_ref[...]) > 0.5

    k_t = jnp.concatenate([r[0, 0, 0].reshape(kw, PAGE_SIZE) for r in page_refs], axis=1).astype(BF16)
    v_t = jnp.concatenate([r[0, 0, 1].reshape(kw, PAGE_SIZE) for r in page_refs], axis=1).astype(BF16)
    new_ok = sel[:, cur:cur + 1]
    o_slc, _ = attend(k_t, v_t, sel_keys, new_row(2 * kw), new_row(3 * kw), new_ok)

    wb = win_ref.shape[-1]
    wpos = past - wb + lax.broadcasted_iota(jnp.int32, (1, wb), 1)
    wmask = (wpos <= q_pos) & (wpos > q_pos - WINDOW) & (wpos >= 0)
    o_win, _ = attend(win_ref[0, si, 0].reshape(kw, wb).astype(BF16), win_ref[0, si, 1].reshape(kw, wb).astype(BF16),
                      wmask, new_row(4 * kw), new_row(5 * kw), True)

    gates = jax.nn.sigmoid(gp_ref[si] + bg_ref[...])
    o_ref[si] = gates[:, 0:1] * o_cmp + gates[:, 1:2] * o_slc + gates[:, 2:3] * o_win


def nsa_sample_layer(x, cache_cmp_t, cache_slc_t, cache_win_t, layer, page_table, w_in, b_gate, w_cmp, pe_cmp,
                     w_out, g, b):
    bsz, d = x.shape
    n_pages = page_table.shape[1]
    past = n_pages * PAGE_SIZE
    w2 = 2 * KV_WIDTH
    n_in = w_in.shape[1]
    n_pad = -(-n_in // LANES) * LANES
    w_pad = jnp.concatenate([w_in, jnp.zeros((d, n_pad - n_in), w_in.dtype)], axis=1).astype(BF16)
    proj = linear(x, w_pad, tm=bsz, tn=n_pad)
    q = proj[:, :Q_WIDTH].reshape(bsz, NSA_HEADS, HEAD_DIM)
    kv_new = proj[:, Q_WIDTH:Q_WIDTH + 3 * w2]
    gate_pre = proj[:, Q_WIDTH + 3 * w2:n_in].reshape(bsz, 3, NSA_HEADS).transpose(0, 2, 1)
    bg = b_gate.reshape(3, NSA_HEADS).T

    kct, vct = nsa_compress_sample(cache_cmp_t, layer, page_table, w_cmp, pe_cmp)
    ncp = kct.shape[2]
    n_cmp = (past - CMP_LEN) // CMP_STRIDE + 1
    assert (past + 1 - CMP_LEN) // CMP_STRIDE + 1 == n_cmp
    n_slc = -(-(past + 1) // SLC_BLOCK)
    nsp = -(-n_slc // LANES) * LANES
    cmap = jnp.asarray(_cmap_t(n_cmp, n_slc, nsp, ncp).T, BF16)
    heads = np.arange(NSA_HEADS)
    g16 = jnp.asarray(heads[:, None] // NSA_GROUP == heads[None, :] // NSA_GROUP, BF16)
    expand = jnp.asarray(np.arange(nsp)[:, None] == np.arange(past)[None, :] // SLC_BLOCK, BF16)

    wb = cache_win_t.shape[-1]
    const = lambda a: pl.BlockSpec(a.shape, lambda b, pt: (0,) * a.ndim)
    n_seq = SAMPLE_SEQS_PER_STEP
    per_b = lambda *shape: pl.BlockSpec((n_seq,) + shape, lambda b, pt: (b,) + (0,) * len(shape))
    page_specs = _page_specs(cache_slc_t, layer, n_seq, n_pages)
    win_spec = pl.BlockSpec((1, n_seq) + cache_win_t.shape[2:], lambda b, pt: (layer, b, 0, 0, 0, 0))
    o = pl.pallas_call(
        functools.partial(_nsa_sample_attn_kernel, past=past, n_cmp=n_cmp, n_slc=n_slc),
        grid_spec=pltpu.PrefetchScalarGridSpec(
            num_scalar_prefetch=1, grid=(bsz // n_seq,),
            in_specs=[per_b(NSA_HEADS, HEAD_DIM), per_b(1, 3 * w2), per_b(NSA_HEADS, 3), const(bg),
                      per_b(KV_WIDTH, ncp), per_b(KV_WIDTH, ncp), win_spec,
                      const(cmap), const(g16), const(expand)] + page_specs,
            out_specs=per_b(NSA_HEADS, HEAD_DIM)),
        out_shape=jax.ShapeDtypeStruct((bsz, NSA_HEADS, HEAD_DIM), F32),
        compiler_params=_params("parallel"),
        name="nsa_sample_attention",
    )(page_table, q, kv_new.reshape(bsz, 1, 3 * w2), gate_pre, bg, kct, vct, cache_win_t, cmap, g16, expand,
      *([cache_slc_t] * (n_seq * n_pages)))
    y = proj_deepnorm(o.reshape(bsz, Q_WIDTH), w_out.astype(BF16), x, g, b, tm=bsz)
    return y, kv_new


HGRN_C = 128
HGRN_SUB = 8
HGRN_LEVELS = (8, 16, 32, 64)


def _hgrn_lower_bound(logits, layer):
    e = jnp.exp(logits - jnp.max(logits, axis=0, keepdims=True))
    sm = e / jnp.sum(e, axis=0, keepdims=True)
    lb = jnp.zeros_like(sm[0])
    for j in range(1, layer + 1):
        lb = lb + sm[j]
    return lb


def _hgrn_gates(f_raw, lb):
    sg = jax.nn.sigmoid(f_raw)
    f = lb + (1.0 - lb) * sg
    log_f = jnp.where(f > 0.0, jnp.log(f), jnp.log1p(-lb) + f_raw)
    return log_f, (1.0 - lb) * (1.0 - sg)


def _rms_gate(o, gain, g_raw):
    o = o * lax.rsqrt(jnp.mean(o * o, axis=-1, keepdims=True) + LN_EPS) * gain
    return o * (g_raw * jax.nn.sigmoid(g_raw))


def _hgrn_prompt_kernel(x_ref, wq_ref, wf_ref, wi_ref, wg_ref, lb_ref, gain_ref, tri_ref, lmask_ref,
                        y_ref, s_ref, st_ref, cum_ref, *, layer):
    c = HGRN_C
    nblk = c // HGRN_SUB
    lb_all = _hgrn_lower_bound(lb_ref[:, 0], layer)
    st_ref[...] = jnp.zeros_like(st_ref)
    rsub = lax.broadcasted_iota(jnp.int32, (c, 1), 0) % HGRN_SUB

    def roll8(a, j):
        return pltpu.roll(a.reshape(nblk, HGRN_SUB, a.shape[1]), j, 1).reshape(a.shape)

    def head_chunk(hi, rows, q, f_raw, v, g_raw):
        lf, k = _hgrn_gates(f_raw, lb_all[:, hi * HGRN_DK:(hi + 1) * HGRN_DK])
        cum = sum(_dot(tri_ref[...], piece) for piece in _split3(lf))
        cum_ref[hi] = cum
        ends = cum_ref[hi, pl.ds(HGRN_SUB - 1, nblk, stride=HGRN_SUB), :]
        end_b = [jnp.broadcast_to(ends[i:i + 1, :], (HGRN_SUB, HGRN_DK)) for i in range(nblk)]
        zero_b = jnp.zeros((HGRN_SUB, HGRN_DK), F32)
        vb = v.astype(BF16)

        a_off = jnp.zeros((c, c), F32)
        for li, lev in enumerate(HGRN_LEVELS):
            per = lev // HGRN_SUB
            before = [end_b[(i // per) * per - 1] if i >= per else zero_b for i in range(nblk)]
            after = [end_b[(i // per) * per + per - 1] for i in range(nblk)]
            qd = q * jnp.exp(cum - jnp.concatenate(before, axis=0))
            kd = k * jnp.exp(jnp.concatenate(after, axis=0) - cum)
            a_off = a_off + _dot_nt(qd.astype(BF16), kd.astype(BF16)) * lmask_ref[li]
        o = _dot(a_off.astype(BF16), vb)

        for j in range(HGRN_SUB):
            if j == 0:
                w = q * k
                vj = v
            else:
                ok = rsub >= j
                w = jnp.where(ok, q * roll8(k, j) * jnp.exp(cum - roll8(cum, j)), 0.0)
                vj = roll8(v, j)
            o = o + jnp.sum(w, axis=1, keepdims=True) * vj

        last = end_b[nblk - 1][0:1, :]
        o = o + _dot_nt((q * jnp.exp(cum)).astype(BF16), st_ref[hi].astype(BF16))
        kd = (k * jnp.exp(last - cum)).astype(BF16)
        st_ref[hi] = st_ref[hi] * jnp.exp(last) + _dot(v.T.astype(BF16), kd)
        y_ref[0, rows, hi * HGRN_DV:(hi + 1) * HGRN_DV] = _rms_gate(o, gain_ref[...], g_raw)

    n_heads = st_ref.shape[0]

    def chunk(ci, _):
        rows = pl.ds(pl.multiple_of(ci * c, c), c)
        xb = x_ref[0, rows, :].astype(BF16)
        q, f_raw, v, g_raw = (_dot(xb, w[...]) for w in (wq_ref, wf_ref, wi_ref, wg_ref))
        for hi in range(n_heads):
            cols = slice(hi * HGRN_DK, (hi + 1) * HGRN_DK)
            head_chunk(hi, rows, q[:, cols], f_raw[:, cols], v[:, cols], g_raw[:, cols])
        return 0

    lax.fori_loop(0, x_ref.shape[1] // c, chunk, 0)
    for hi in range(n_heads):
        s_ref[0, hi] = st_ref[hi].T


HGRN_HEADS_PER_STEP = 4


def hgrn_prompt(x, w_in_bf, lb_logits, gain, layer):
    bsz, t, d = x.shape
    c = HGRN_C
    hh = HGRN_HEADS
    hps = HGRN_HEADS_PER_STEP
    wide = hps * HGRN_DK
    nl = lb_logits.shape[0]
    tri = jnp.asarray(np.tril(np.ones((c, c), np.float32)), BF16)
    ti = np.arange(c)[:, None]
    si = np.arange(c)[None, :]
    lmask = jnp.asarray(np.stack([((ti // (2 * lev) == si // (2 * lev)) & ((ti // lev) % 2 == 1)
                                   & ((si // lev) % 2 == 0)) for lev in HGRN_LEVELS]), F32)
    w_cols = lambda part: pl.BlockSpec((d, wide), lambda b, h: (0, part * (hh // hps) + h))
    return pl.pallas_call(
        functools.partial(_hgrn_prompt_kernel, layer=layer),
        grid=(bsz, hh // hps),
        in_specs=[pl.BlockSpec((1, t, d), lambda b, h: (b, 0, 0)),
                  w_cols(0), w_cols(1), w_cols(2), w_cols(3),
                  pl.BlockSpec((nl, 1, 1, wide), lambda b, h: (0, h, 0, 0)),
                  pl.BlockSpec((1, HGRN_DV), lambda b, h: (0, 0)),
                  pl.BlockSpec((c, c), lambda b, h: (0, 0)),
                  pl.BlockSpec((len(HGRN_LEVELS), c, c), lambda b, h: (0, 0, 0))],
        out_specs=(pl.BlockSpec((1, t, hps * HGRN_DV), lambda b, h: (b, 0, h)),
                   pl.BlockSpec((1, hps, HGRN_DK, HGRN_DV), lambda b, h: (b, h, 0, 0))),
        out_shape=(jax.ShapeDtypeStruct((bsz, t, hh * HGRN_DV), F32),
                   jax.ShapeDtypeStruct((bsz, hh, HGRN_DK, HGRN_DV), F32)),
        scratch_shapes=[pltpu.VMEM((hps, HGRN_DV, HGRN_DK), F32), pltpu.VMEM((hps, c, HGRN_DK), F32)],
        compiler_params=_params("parallel", "arbitrary"),
        name="hgrn_prompt",
    )(x, w_in_bf, w_in_bf, w_in_bf, w_in_bf, lb_logits.reshape(nl, hh // hps, 1, wide),
      gain.reshape(1, HGRN_DV), tri, lmask)


def _hgrn_sample_kernel(p_ref, s0_ref, lb_ref, gain_ref, y_ref, s_ref, *, layer):
    hh = HGRN_HEADS
    n_seq = p_ref.shape[0]
    lb = _hgrn_lower_bound(lb_ref[...], layer)
    parts, rows = [], []
    for si in range(n_seq):
        p = p_ref[si]
        q8, f8, i8, g8 = (p[j * hh:(j + 1) * hh, :] for j in range(4))
        lf8, k8 = _hgrn_gates(f8, lb)
        parts.append((q8, k8, i8, g8))
        rows.append(jnp.exp(lf8))
    pad = [jnp.zeros((LANES - hh * n_seq, HGRN_DK), F32)] if hh * n_seq < LANES else []
    cols = jnp.concatenate(rows + pad, axis=0).T
    zeros = jnp.zeros((BF16_ROWS - 1, HGRN_DK), BF16)
    head_row = lax.broadcasted_iota(jnp.int32, (hh, 1), 0)
    for si in range(n_seq):
        q8, k8, i8, g8 = parts[si]
        q8b = q8.astype(BF16)
        for h in range(hh):
            k16 = jnp.concatenate([k8[h:h + 1, :].astype(BF16), zeros], axis=0)
            i16 = jnp.concatenate([i8[h:h + 1, :].astype(BF16), zeros], axis=0)
            outer = lax.dot_general(k16, i16, (((0,), (0,)), ((), ())), preferred_element_type=F32)
            s_new = cols[:, hh * si + h:hh * si + h + 1] * s0_ref[si, h] + outer
            s_ref[si, h] = s_new
            o_h = _dot(q8b, s_new.astype(BF16))
            o = o_h if h == 0 else jnp.where(head_row == h, o_h, o)
        y_ref[si] = _rms_gate(o, gain_ref[...], g8)


HGRN_SAMPLE_SEQS = 8


def hgrn_sample(proj, s0_all, j, lb_logits, gain, layer):
    bsz = proj.shape[0]
    hh = HGRN_HEADS
    nl = lb_logits.shape[0]
    n_seq = HGRN_SAMPLE_SEQS
    assert bsz % n_seq == 0
    y, s_new = pl.pallas_call(
        functools.partial(_hgrn_sample_kernel, layer=layer),
        grid=(bsz // n_seq,),
        in_specs=[pl.BlockSpec((n_seq, 4 * hh, HGRN_DK), lambda b: (b, 0, 0)),
                  pl.BlockSpec((None, n_seq, hh, HGRN_DK, HGRN_DV), lambda b: (j, b, 0, 0, 0)),
                  pl.BlockSpec((nl, hh, HGRN_DK), lambda b: (0, 0, 0)),
                  pl.BlockSpec((1, HGRN_DV), lambda b: (0, 0))],
        out_specs=(pl.BlockSpec((n_seq, hh, HGRN_DV), lambda b: (b, 0, 0)),
                   pl.BlockSpec((n_seq, hh, HGRN_DK, HGRN_DV), lambda b: (b, 0, 0, 0))),
        out_shape=(jax.ShapeDtypeStruct((bsz, hh, HGRN_DV), F32),
                   jax.ShapeDtypeStruct(s0_all.shape[1:], F32)),
        compiler_params=_params("parallel"),
        name="hgrn_sample",
    )(proj.reshape(bsz, 4 * hh, HGRN_DK), s0_all, lb_logits.reshape(nl, hh, HGRN_DK), gain.reshape(1, HGRN_DV))
    return y.reshape(bsz, hh * HGRN_DV), s_new


TM_PROJ = 512
TM_FFN = 1024
FFN_HC = 256
TM_GMLP = 256
TM_LINEAR = 512
TN_LINEAR = 1024


def kernel(x_prompt, x_sample, cache_cmp_kv, cache_slc_kv, cache_win_kv, state_hgrn, page_table, ln_gain, ln_bias, ffn_w_in, ffn_w_out, nsa_w_in, nsa_b_gate, nsa_w_cmp, nsa_pe_cmp, nsa_w_out, gmlp_w_in, gmlp_b_in, gmlp_ln_v, gmlp_w_sp, gmlp_b_sp, gmlp_w_out, hgrn_w_in, hgrn_lb_logits, hgrn_norm_gain, hgrn_w_out):
    bsz, t, d = x_prompt.shape
    sb = x_sample.shape[0]
    assert x_sample.shape[1] == 1
    xp = x_prompt
    xs = x_sample.reshape(sb, d)
    kv_shape = (2, NSA_KV_HEADS, HEAD_DIM)
    wb_prompt = min(WINDOW, t)
    cmp_t, slc_t, win_t = _token_minor(cache_cmp_kv), _token_minor(cache_slc_kv), _token_minor(cache_win_kv)
    cmp_s, slc_s, win_s, gv_s, hs_p, hs_s = [], [], [], [], [], []
    n_nsa = (DEPTH + N_MIXERS - 1) // N_MIXERS
    kv_pages = None
    ffn_in_bf, ffn_out_bf = ffn_w_in.astype(BF16), ffn_w_out.astype(BF16)
    for layer in range(DEPTH):
        kind, j = layer % N_MIXERS, layer // N_MIXERS
        g0, b0 = ln_gain[layer, 0], ln_bias[layer, 0]
        if kind == 0:
            xp, kv_pages = nsa_prompt_layer(xp, nsa_w_in[j], nsa_b_gate[j], nsa_w_cmp[j], nsa_pe_cmp[j],
                                            nsa_w_out[j], g0, b0, j, n_nsa, kv_pages)
            xs, kv_new = nsa_sample_layer(xs, cmp_t, slc_t, win_t, j, page_table,
                                          nsa_w_in[j], nsa_b_gate[j], nsa_w_cmp[j], nsa_pe_cmp[j], nsa_w_out[j],
                                          g0, b0)
            w2 = 2 * KV_WIDTH
            cmp_s.append(kv_new[:, 0:w2].reshape((sb, 1) + kv_shape))
            slc_s.append(kv_new[:, w2:2 * w2].reshape((sb, 1) + kv_shape))
            win_s.append(kv_new[:, 2 * w2:3 * w2].reshape((sb, 1) + kv_shape))
        elif kind == 1:
            w_in_bf, w_out_bf = gmlp_w_in[j].astype(BF16), gmlp_w_out[j].astype(BF16)
            xp = gmlp_layer(xp.reshape(bsz * t, d), w_in_bf, gmlp_b_in[j], gmlp_ln_v[j], gmlp_w_sp[j], gmlp_b_sp[j],
                            w_out_bf, g0, b0, chunked=True, tm=TM_GMLP).reshape(bsz, t, d)
            xs, v_new = gmlp_layer(xs, w_in_bf, gmlp_b_in[j], gmlp_ln_v[j], gmlp_w_sp[j], gmlp_b_sp[j],
                                   w_out_bf, g0, b0, chunked=False, tm=sb)
            gv_s.append(v_new.reshape(sb, 1, GMLP_WIDTH))
        else:
            w_in_bf, w_out_bf = hgrn_w_in[j].astype(BF16), hgrn_w_out[j].astype(BF16)
            y, s_p = hgrn_prompt(xp, w_in_bf, hgrn_lb_logits, hgrn_norm_gain[j], layer)
            xp = proj_deepnorm(y.reshape(bsz * t, HGRN_WIDTH), w_out_bf, xp.reshape(bsz * t, d), g0, b0,
                               TM_PROJ).reshape(bsz, t, d)
            proj_s = linear(xs, w_in_bf, sb, TN_LINEAR)
            y_s, s_s = hgrn_sample(proj_s, state_hgrn, j, hgrn_lb_logits, hgrn_norm_gain[j], layer)
            xs = proj_deepnorm(y_s, w_out_bf, xs, g0, b0, sb)
            hs_p.append(s_p)
            hs_s.append(s_s)
        g1, b1 = ln_gain[layer, 1], ln_bias[layer, 1]
        xp = ffn_deepnorm(xp.reshape(bsz * t, d), ffn_in_bf, ffn_out_bf, layer, g1, b1, TM_FFN, FFN_HC).reshape(bsz, t, d)
        xs = ffn_deepnorm(xs, ffn_in_bf, ffn_out_bf, layer, g1, b1, sb, FFN_HC)
    kvc, kvs, kvw = kv_pages
    n_page = t // PAGE_SIZE
    pages = lambda a: _token_major(a.reshape((n_nsa, bsz * n_page) + a.shape[3:]))
    win_rows = _token_major(kvw[:, :, n_page - wb_prompt // PAGE_SIZE:])
    win_p = win_rows.reshape((n_nsa, bsz, wb_prompt) + kv_shape)
    return (xp, xs.reshape(sb, 1, d), pages(kvc), jnp.stack(cmp_s), pages(kvs), jnp.stack(slc_s),
            win_p, jnp.stack(win_s), jnp.stack(gv_s), jnp.stack(hs_p), jnp.stack(hs_s))
```
